```python
import math
import jax, jax.numpy as jnp
from jax import lax
import numpy as np

D_MODEL = 1024
BATCH = 2
SEQ = 8192
DEPTH = 2
DEC_BATCH = 32
DEC_SEQ = 16
PAST_LEN = 2048

CHUNK = 64
GDN_HEADS = 4
GDN_DK = 128
GDN_DV = 128
GDN_QK_W = GDN_HEADS * GDN_DK
GDN_V_W = GDN_HEADS * GDN_DV
GDN_CONV_CH = 2 * GDN_QK_W + GDN_V_W
CONV_W = 4
MLP_GROUPS = 4
MLP_GROUP_W = 128
MLP_W = MLP_GROUPS * MLP_GROUP_W
MLP_CHUNK = 128
SB_HEADS = 4
SB_DH = 128
SB_W = SB_HEADS * SB_DH
SB_BLOCK = 128
N_BRANCH = 3
FFN_HIDDEN = -(-8 * D_MODEL // (3 * 256)) * 256
PLE_DIM = 256
EPS = 1e-6
IN_SIZES = (GDN_CONV_CH, GDN_V_W, GDN_HEADS, GDN_HEADS, 2 * MLP_W, 3 * SB_W, N_BRANCH * D_MODEL)
W_IN_COLS = GDN_CONV_CH + GDN_V_W + 2 * GDN_HEADS + 2 * MLP_W + 3 * SB_W + N_BRANCH * D_MODEL

kernel_name = 'hybrid_gdn_gmlp_stickbreak_stream_step'

F32 = jnp.float32


def rmsnorm(x, g):
    xf = x.astype(F32)
    y = xf * lax.rsqrt(jnp.mean(xf * xf, axis=-1, keepdims=True) + EPS)
    return (y * g.astype(F32)).astype(x.dtype)


def layernorm(x, g, b):
    xf = x.astype(F32)
    mu = jnp.mean(xf, axis=-1, keepdims=True)
    xc = xf - mu
    y = xc * lax.rsqrt(jnp.mean(xc * xc, axis=-1, keepdims=True) + EPS)
    return (y * g.astype(F32) + b.astype(F32)).astype(x.dtype)


def l2norm(x):
    return x * lax.rsqrt(jnp.sum(x * x, axis=-1, keepdims=True) + EPS)


def split_in(proj):
    idx = []
    acc = 0
    for s in IN_SIZES[:-1]:
        acc += s
        idx.append(acc)
    return jnp.split(proj, idx, axis=-1)


def causal_dwconv(x, buf, w):
    xp = jnp.concatenate([buf.astype(x.dtype), x], axis=1)
    c = x.shape[-1]
    y = lax.conv_general_dilated(xp, w.astype(x.dtype)[:, None, :], window_strides=(1,), padding='VALID',
                                 dimension_numbers=('NWC', 'WIO', 'NWC'), feature_group_count=c)
    return y, xp[:, -(CONV_W - 1):]


def gated_delta_rule(q, k, v, g, beta, s0):
    L = q.shape[2]
    q, k, v = (t.transpose(0, 1, 3, 2, 4) for t in (q, k, v))
    gc = jnp.cumsum(g.transpose(0, 1, 3, 2), axis=-1)
    beta = beta.transpose(0, 1, 3, 2)
    incl = jnp.tril(jnp.ones((L, L), dtype=bool))
    strict = jnp.tril(jnp.ones((L, L), dtype=bool), -1)
    decay = jnp.exp(jnp.where(incl, gc[..., :, None] - gc[..., None, :], -jnp.inf))
    kk = jnp.einsum('bnhtk,bnhsk->bnhts', k, k)
    a_mat = jnp.where(strict, beta[..., None] * decay * kk, 0.0)
    rhs = jnp.concatenate([beta[..., None] * v, (beta * jnp.exp(gc))[..., None] * k], axis=-1)
    sol = lax.linalg.triangular_solve(jnp.eye(L, dtype=F32) + a_mat, rhs, left_side=True,
                                      lower=True, unit_diagonal=True)
    w_val, w_key = sol[..., :GDN_DV], sol[..., GDN_DV:]
    p_mat = decay * jnp.einsum('bnhtk,bnhsk->bnhts', q, k)
    q_dec = jnp.exp(gc)[..., None] * q
    g_last = gc[..., -1:]
    k_dec = jnp.exp(g_last - gc)[..., None] * k
    blk_decay = jnp.exp(g_last[..., 0])
    xs = tuple(jnp.moveaxis(t, 1, 0) for t in (w_val, w_key, p_mat, q_dec, k_dec, blk_decay))

    def step(s, xs_n):
        wv, wk, pm, qd, kd, bd = xs_n
        u = wv - jnp.einsum('bhlk,bhkv->bhlv', wk, s)
        o = jnp.einsum('bhlk,bhkv->bhlv', qd, s) + jnp.einsum('bhts,bhsv->bhtv', pm, u)
        s = bd[..., None, None] * s + jnp.einsum('bhlk,bhlv->bhkv', kd, u)
        return s, o

    s_fin, o = lax.scan(step, s0, xs)
    o = jnp.moveaxis(o, 0, 1).transpose(0, 1, 3, 2, 4)
    return o, s_fin


def gdn_branch(qkv_in, z, a, b, conv_buf, s0, conv_w, a_log, dt_bias, norm_g):
    B, T, _ = qkv_in.shape
    qkv, conv_new = causal_dwconv(qkv_in, conv_buf, conv_w)
    qkv = jax.nn.silu(qkv.astype(F32))
    q, k, v = jnp.split(qkv, [GDN_QK_W, 2 * GDN_QK_W], axis=-1)
    q = l2norm(q.reshape(B, T, GDN_HEADS, GDN_DK)) * (GDN_DK ** -0.5)
    k = l2norm(k.reshape(B, T, GDN_HEADS, GDN_DK))
    v = v.reshape(B, T, GDN_HEADS, GDN_DV)
    g = -jnp.exp(a_log.astype(F32)) * jax.nn.softplus(a.astype(F32) + dt_bias.astype(F32))
    beta = jax.nn.sigmoid(b.astype(F32))
    L = min(CHUNK, T)
    N = T // L
    o, s_new = gated_delta_rule(q.reshape(B, N, L, GDN_HEADS, GDN_DK), k.reshape(B, N, L, GDN_HEADS, GDN_DK),
                                v.reshape(B, N, L, GDN_HEADS, GDN_DV), g.reshape(B, N, L, GDN_HEADS),
                                beta.reshape(B, N, L, GDN_HEADS), s0.astype(F32))
    o = rmsnorm(o.reshape(B, T, GDN_HEADS, GDN_DV), norm_g)
    o = o * jax.nn.silu(z.astype(F32)).reshape(B, T, GDN_HEADS, GDN_DV)
    return o.reshape(B, T, GDN_V_W).astype(qkv_in.dtype), conv_new, s_new.astype(qkv_in.dtype)


def gmlp_branch(uv, ln_g, ln_b, w_s, b_s):
    B, T, _ = uv.shape
    u, v = jnp.split(jax.nn.gelu(uv.astype(F32), approximate=False), 2, axis=-1)
    v = layernorm(v, ln_g, ln_b)
    L = min(MLP_CHUNK, T)
    N = T // L
    pos = jnp.arange(L)
    mask = (pos[:, None] // CHUNK) >= (pos[None, :] // CHUNK)
    w = jnp.where(mask[None], w_s[:, :L, :L].astype(F32), 0.0)
    s = jnp.einsum('gts,bnsgc->bntgc', w, v.reshape(B, N, L, MLP_GROUPS, MLP_GROUP_W))
    s = s + b_s[:, :L].astype(F32).T[None, None, :, :, None]
    y = u * s.reshape(B, T, MLP_W)
    return y.astype(uv.dtype), v.astype(uv.dtype)


def sb_attend(q, k, v, q_pos, k_pos):
    z = jnp.einsum('bthd,bshd->bhts', q, k) * (SB_DH ** -0.5)
    mask = k_pos[None, :] < q_pos[:, None]
    log_fail = jnp.where(mask, -jax.nn.softplus(z), 0.0)
    after = lax.cumsum(log_fail, axis=3, reverse=True) - log_fail
    a = jnp.where(mask, jnp.exp(jax.nn.log_sigmoid(z) + after), 0.0)
    return jnp.einsum('bhts,bshd->bthd', a, v)


def sb_prompt(q, k, v):
    B, T, H, D = q.shape
    nb = T // SB_BLOCK
    pos = jnp.arange(T, dtype=jnp.int32)
    kf, vf = k.astype(F32), v.astype(F32)
    qb = jnp.moveaxis(q.astype(F32).reshape(B, nb, SB_BLOCK, H, D), 1, 0)
    out = lax.map(lambda blk: sb_attend(blk[0], kf, vf, blk[1], pos), (qb, pos.reshape(nb, SB_BLOCK)))
    return jnp.moveaxis(out, 0, 1).reshape(B, T, H, D)


def sb_sample(q, k, v, k_past, v_past):
    P = k_past.shape[1]
    T = q.shape[1]
    k_all = jnp.concatenate([k_past.astype(F32), k.astype(F32)], axis=1)
    v_all = jnp.concatenate([v_past.astype(F32), v.astype(F32)], axis=1)
    k_pos = jnp.arange(P + T, dtype=jnp.int32)
    q_pos = P + jnp.arange(T, dtype=jnp.int32)
    return sb_attend(q.astype(F32), k_all, v_all, q_pos, k_pos)


def trunk_layer(h, p, conv_buf, s0, k_past, v_past,
                norm_mix, w_in, conv_w, a_log, dt_bias, gdn_norm, w_pa,
                ln_v_g, ln_v_b, w_s, b_s, w_pb, w_pc, w_o,
                norm_ffn, w_ffn_in, w_ffn_out, norm_ple, w_ple_gate, w_ple_proj):
    B, T, _ = h.shape
    dt = h.dtype
    xn = rmsnorm(h, norm_mix)
    a_qkv, a_z, a_a, a_b, b_uv, c_qkv, gate_logits = split_in(xn @ w_in)
    y_a, conv_new, s_new = gdn_branch(a_qkv, a_z, a_a, a_b, conv_buf, s0, conv_w, a_log, dt_bias, gdn_norm)
    y_b, v_rows = gmlp_branch(b_uv, ln_v_g, ln_v_b, w_s, b_s)
    q_c, k_c, v_c = [t.reshape(B, T, SB_HEADS, SB_DH) for t in jnp.split(c_qkv, 3, axis=-1)]
    if k_past is None:
        y_c = sb_prompt(q_c, k_c, v_c)
    else:
        y_c = sb_sample(q_c, k_c, v_c, k_past, v_past)
    y_c = y_c.reshape(B, T, SB_W).astype(dt)
    gates = jax.nn.sigmoid(gate_logits.astype(F32)).astype(dt).reshape(B, T, N_BRANCH, D_MODEL)
    merged = gates[:, :, 0] * (y_a @ w_pa) + gates[:, :, 1] * (y_b @ w_pb) + gates[:, :, 2] * (y_c @ w_pc)
    h = h + merged @ w_o
    a1, a3 = jnp.split(rmsnorm(h, norm_ffn) @ w_ffn_in, 2, axis=-1)
    h = h + (jax.nn.silu(a1) * a3) @ w_ffn_out
    h = h + jax.nn.sigmoid(rmsnorm(h, norm_ple) @ w_ple_gate) * (p.astype(dt) @ w_ple_proj)
    return h, conv_new, s_new, k_c, v_c, v_rows


def setup_inputs(seed: int = 0) -> dict:
    key = jax.random.key(seed)
    ks = iter(jax.random.split(key, 40))

    def nrm(shape, scale=1.0):
        return jax.random.normal(next(ks), shape, F32) * scale

    def gain(shape):
        return 1.0 + nrm(shape, 0.02)

    dt0 = jnp.exp(jax.random.uniform(next(ks), (DEPTH, GDN_HEADS), F32, math.log(1e-3), math.log(1e-1)))
    return {
        'x_prompt': nrm((BATCH, SEQ, D_MODEL)),
        'x_sample': nrm((DEC_BATCH, DEC_SEQ, D_MODEL)),
        'state_gdn_conv': nrm((DEPTH, DEC_BATCH, CONV_W - 1, GDN_CONV_CH)),
        'state_gdn_s': nrm((DEPTH, DEC_BATCH, GDN_HEADS, GDN_DK, GDN_DV), 0.05),
        'cache_sb_k': nrm((DEPTH, DEC_BATCH, PAST_LEN, SB_HEADS, SB_DH)),
        'cache_sb_v': nrm((DEPTH, DEC_BATCH, PAST_LEN, SB_HEADS, SB_DH)),
        'p_prompt': nrm((DEPTH, BATCH, SEQ, PLE_DIM)),
        'p_sample': nrm((DEPTH, DEC_BATCH, DEC_SEQ, PLE_DIM)),
        'norm_mix': gain((DEPTH, D_MODEL)),
        'w_in': nrm((DEPTH, D_MODEL, W_IN_COLS), D_MODEL ** -0.5),
        'conv_w': nrm((DEPTH, CONV_W, GDN_CONV_CH), CONV_W ** -0.5),
        'a_log': jnp.log(jax.random.uniform(next(ks), (DEPTH, GDN_HEADS), F32, 1.0, 16.0)),
        'dt_bias': dt0 + jnp.log(-jnp.expm1(-dt0)),
        'gdn_norm': gain((DEPTH, GDN_DV)),
        'w_pa': nrm((DEPTH, GDN_V_W, D_MODEL), GDN_V_W ** -0.5),
        'ln_v_g': gain((DEPTH, MLP_W)),
        'ln_v_b': nrm((DEPTH, MLP_W), 0.02),
        'w_s': nrm((DEPTH, MLP_GROUPS, MLP_CHUNK, MLP_CHUNK), MLP_CHUNK ** -0.5),
        'b_s': gain((DEPTH, MLP_GROUPS, MLP_CHUNK)),
        'w_pb': nrm((DEPTH, MLP_W, D_MODEL), MLP_W ** -0.5),
        'w_pc': nrm((DEPTH, SB_W, D_MODEL), SB_W ** -0.5),
        'w_o': nrm((DEPTH, D_MODEL, D_MODEL), D_MODEL ** -0.5),
        'norm_ffn': gain((DEPTH, D_MODEL)),
        'w_ffn_in': nrm((DEPTH, D_MODEL, 2 * FFN_HIDDEN), D_MODEL ** -0.5),
        'w_ffn_out': nrm((DEPTH, FFN_HIDDEN, D_MODEL), FFN_HIDDEN ** -0.5),
        'norm_ple': gain((DEPTH, D_MODEL)),
        'w_ple_gate': nrm((DEPTH, D_MODEL, D_MODEL), D_MODEL ** -0.5),
        'w_ple_proj': nrm((DEPTH, PLE_DIM, D_MODEL), PLE_DIM ** -0.5),
        'norm_final': gain((D_MODEL,)),
    }


def reference(x_prompt, x_sample, state_gdn_conv, state_gdn_s, cache_sb_k, cache_sb_v, p_prompt, p_sample,
              norm_mix, w_in, conv_w, a_log, dt_bias, gdn_norm, w_pa, ln_v_g, ln_v_b, w_s, b_s,
              w_pb, w_pc, w_o, norm_ffn, w_ffn_in, w_ffn_out, norm_ple, w_ple_gate, w_ple_proj, norm_final):
    hp, hs = x_prompt, x_sample
    bp = x_prompt.shape[0]
    pc_l, ps_l, pk_l, pv_l = [], [], [], []
    sc_l, ss_l, sk_l, sv_l, sm_l = [], [], [], [], []
    for i in range(DEPTH):
        lw = (norm_mix[i], w_in[i], conv_w[i], a_log[i], dt_bias[i], gdn_norm[i], w_pa[i],
              ln_v_g[i], ln_v_b[i], w_s[i], b_s[i], w_pb[i], w_pc[i], w_o[i],
              norm_ffn[i], w_ffn_in[i], w_ffn_out[i], norm_ple[i], w_ple_gate[i], w_ple_proj[i])
        buf0 = jnp.zeros((bp, CONV_W - 1, GDN_CONV_CH), x_prompt.dtype)
        s00 = jnp.zeros((bp, GDN_HEADS, GDN_DK, GDN_DV), F32)
        hp, pc, ps, pk, pv, _ = trunk_layer(hp, p_prompt[i], buf0, s00, None, None, *lw)
        hs, sc, ss, sk, sv, sm = trunk_layer(hs, p_sample[i], state_gdn_conv[i], state_gdn_s[i],
                                             cache_sb_k[i], cache_sb_v[i], *lw)
        pc_l.append(pc); ps_l.append(ps); pk_l.append(pk); pv_l.append(pv)
        sc_l.append(sc); ss_l.append(ss); sk_l.append(sk); sv_l.append(sv); sm_l.append(sm)
    y_prompt = rmsnorm(hp, norm_final)
    y_sample = rmsnorm(hs, norm_final)
    new_gdn_conv_prompt = jnp.stack(pc_l)
    new_gdn_s_prompt = jnp.stack(ps_l)
    new_sb_k_prompt = jnp.stack(pk_l)
    new_sb_v_prompt = jnp.stack(pv_l)
    new_gdn_conv_sample = jnp.stack(sc_l)
    new_gdn_s_sample = jnp.stack(ss_l)
    new_sb_k_sample = jnp.stack(sk_l)
    new_sb_v_sample = jnp.stack(sv_l)
    new_mlp_v_sample = jnp.stack(sm_l)
    return (y_prompt, y_sample, new_gdn_conv_prompt, new_gdn_s_prompt, new_sb_k_prompt, new_sb_v_prompt,
            new_gdn_conv_sample, new_gdn_s_sample, new_sb_k_sample, new_sb_v_sample, new_mlp_v_sample)
```

```python
import functools
import math

import jax
import jax.numpy as jnp
from jax import lax
from jax.experimental import pallas as pl
from jax.experimental.pallas import tpu as pltpu

F32 = jnp.float32
BF16 = jnp.bfloat16

EPS = 1e-6
CHUNK = 64
LANES = 128
ROWS = 128
SB_BLOCK = 256
SB_DEAD_LOG = -104.0
VMEM_LIMIT = 56 * 1024 * 1024


def _cparams(sem):
    return pltpu.CompilerParams(dimension_semantics=sem, vmem_limit_bytes=VMEM_LIMIT)


def _resident(shape):
    nd = len(shape)
    return pl.BlockSpec(shape, lambda *_: (0,) * nd, pipeline_mode=pl.Buffered(1))


def _rms(x, g):
    return x * lax.rsqrt(jnp.mean(x * x, axis=-1, keepdims=True) + EPS) * g


def _dot(a, b):
    return jnp.dot(a.astype(BF16), b.astype(BF16), preferred_element_type=F32)


def _dot_nt(a, b):
    return lax.dot_general(a.astype(BF16), b.astype(BF16), (((1,), (1,)), ((), ())),
                           preferred_element_type=F32)


def _dot_tn(a, b):
    return lax.dot_general(a.astype(BF16), b.astype(BF16), (((0,), (0,)), ((), ())),
                           preferred_element_type=F32)


def _split(x):
    hi = x.astype(BF16)
    lo = (x - hi.astype(F32)).astype(BF16)
    return hi, lo


def _dot3(a, b):
    ah, al = _split(a)
    bh, bl = _split(b)
    d = functools.partial(jnp.dot, preferred_element_type=F32)
    return d(ah, bh) + (d(al, bh) + d(ah, bl))


def _softplus(x):
    return jnp.maximum(x, 0.0) + jnp.log(1.0 + jnp.exp(-jnp.abs(x)))


def _silu(x):
    return x * jax.nn.sigmoid(x)


def _in_proj_body(h_ref, g_ref, wa_ref, wab_ref, wb_ref, wc_ref, wg_ref,
                  aqkv_ref, az_ref, ab_ref, buv_ref, q_ref, k_ref, v_ref, gate_ref):
    xn = _rms(h_ref[...], g_ref[...]).astype(BF16)
    d = functools.partial(jnp.dot, preferred_element_type=F32)
    n_qkv = aqkv_ref.shape[1]
    r = d(xn, wa_ref[...])
    aqkv_ref[...] = r[:, :n_qkv]
    az_ref[...] = r[:, n_qkv:]
    ab_ref[...] = d(xn, wab_ref[...])
    buv_ref[...] = d(xn, wb_ref[...])
    r = d(xn, wc_ref[...])
    w = q_ref.shape[1]
    q_ref[...] = r[:, :w]
    k_ref[...] = r[:, w:2 * w]
    v_ref[...] = r[:, 2 * w:]
    gate_ref[...] = d(xn, wg_ref[...])


def _in_proj(h, norm_g, lw, tm):
    ntok, dm = h.shape
    tm = min(tm, ntok)
    widths = (lw['n_qkv'], lw['n_z'], LANES, lw['n_uv'], lw['n_sb'], lw['n_sb'], lw['n_sb'], lw['n_gate'])
    tok = lambda w: pl.BlockSpec((tm, w), lambda i: (i, 0))
    return pl.pallas_call(
        _in_proj_body,
        grid=(ntok // tm,),
        in_specs=[tok(dm), _resident((1, dm)), _resident(lw['w_a'].shape), _resident(lw['w_ab'].shape),
                  _resident(lw['w_b'].shape), _resident(lw['w_c'].shape), _resident(lw['w_g'].shape)],
        out_specs=[tok(w) for w in widths],
        out_shape=[jax.ShapeDtypeStruct((ntok, w), F32) for w in widths],
        compiler_params=_cparams(("parallel",)),
        name="in_proj",
    )(h, norm_g, lw['w_a'], lw['w_ab'], lw['w_b'], lw['w_c'], lw['w_g'])


def _gdn_body(qkv_ref, z_ref, ab_ref, cbuf_ref, s0_ref, cw_ref, alog_ref, dtb_ref, gn_ref,
              y_ref, snew_ref, xprev_sc, s_sc, *, t_real, n_heads, n_double):
    t = pl.program_id(1)
    nt = pl.num_programs(1)
    R = ROWS
    qk_w = n_heads * LANES

    @pl.when(t == 0)
    def _():
        xprev_sc[...] = cbuf_ref[...]
        s_sc[...] = s0_ref[...]

    row = lax.broadcasted_iota(jnp.int32, (R, LANES), 0)
    col = lax.broadcasted_iota(jnp.int32, (R, LANES), 1)
    valid = row < t_real

    def pad_rows(x):
        if t_real == R:
            return x
        return jnp.concatenate([x, jnp.zeros((R - t_real, x.shape[1]), x.dtype)], axis=0)

    x = pad_rows(qkv_ref[...])
    prev = xprev_sc[...]
    cw = cw_ref[...]
    n_tap = cw.shape[0]
    row8 = lax.broadcasted_iota(jnp.int32, (8, x.shape[1]), 0)
    y = x * cw[n_tap - 1:n_tap, :]
    for sh in range(1, n_tap):
        xs = pltpu.roll(x, sh, 0)
        head = jnp.where(row8 < sh, pltpu.roll(prev, sh, 0), xs[:8])
        xs = jnp.concatenate([head, xs[8:]], axis=0)
        y = y + xs * cw[n_tap - 1 - sh:n_tap - sh, :]
    xprev_sc[...] = x[R - 8:, :]
    y = _silu(y)

    ab = pad_rows(ab_ref[...])
    g = -jnp.exp(alog_ref[...]) * _softplus(ab + dtb_ref[...])
    beta = jax.nn.sigmoid(ab)
    if t_real != R:
        g = jnp.where(valid, g, 0.0)
        beta = jnp.where(valid, beta, 0.0)
    rm = row & (CHUNK - 1)
    gc = g
    sh = 1
    while sh < CHUNK:
        gc = gc + jnp.where(rm >= sh, pltpu.roll(gc, sh, 0), 0.0)
        sh *= 2
    gc_t = gc.T
    e_gc = jnp.exp(gc)
    g_last = jnp.where(row < CHUNK, gc[CHUNK - 1:CHUNK, :], gc[R - 1:R, :])
    e_rest = jnp.exp(g_last - gc)

    same_blk = (row & -CHUNK) == (col & -CHUNK)
    incl = same_blk & (row >= col)
    strict = same_blk & (row > col)
    eye = (row == col).astype(F32)
    zeros_half = jnp.zeros((CHUNK, LANES), F32)
    z_all = z_ref[...]
    gn = gn_ref[...]

    for h in range(n_heads):
        sl = slice(h * LANES, (h + 1) * LANES)
        qh = y[:, sl]
        kh = y[:, qk_w + h * LANES: qk_w + (h + 1) * LANES]
        vh = y[:, 2 * qk_w + h * LANES: 2 * qk_w + (h + 1) * LANES]
        qh = qh * lax.rsqrt(jnp.sum(qh * qh, axis=-1, keepdims=True) + EPS) * (LANES ** -0.5)
        kh = kh * lax.rsqrt(jnp.sum(kh * kh, axis=-1, keepdims=True) + EPS)
        if t_real != R:
            qh = jnp.where(valid, qh, 0.0)
            kh = jnp.where(valid, kh, 0.0)
            vh = jnp.where(valid, vh, 0.0)

        gcol = gc[:, h:h + 1]
        grow = gc_t[h:h + 1, :]
        bcol = beta[:, n_heads + h:n_heads + h + 1]
        ecol = e_gc[:, h:h + 1]
        rcol = e_rest[:, h:h + 1]

        decay = jnp.where(incl, jnp.exp(gcol - grow), 0.0)
        kq = _dot_nt(jnp.concatenate([kh, qh], axis=0), kh)
        a_neg = jnp.where(strict, -(bcol * decay * kq[:R]), 0.0)
        p_mat = decay * kq[R:]

        t_inv = eye + a_neg
        pw = _dot3(a_neg, a_neg) if n_double else None
        for it in range(n_double):
            if it < n_double - 1:
                both = _dot3(jnp.concatenate([pw, t_inv], axis=0), pw)
                pw = both[:R]
                t_inv = t_inv + both[R:]
            else:
                t_inv = t_inv + _dot3(t_inv, pw)

        k_in = ecol * kh
        q_dec = ecol * qh
        k_dec = rcol * kh
        s_cur = s_sc[h]
        outs = []
        for c in range(R // CHUNK):
            rs = slice(c * CHUNK, (c + 1) * CHUNK)
            ks_qs = _dot(jnp.concatenate([k_in[rs], q_dec[rs]], axis=0), s_cur)
            rhs = bcol[rs] * (vh[rs] - ks_qs[:CHUNK])
            parts = [zeros_half] * (R // CHUNK)
            parts[c] = rhs
            u = _dot3(t_inv[rs, :], jnp.concatenate(parts, axis=0))
            parts[c] = u
            u_full = jnp.concatenate(parts, axis=0)
            outs.append(ks_qs[CHUNK:] + _dot(p_mat[rs, :], u_full))
            parts[c] = k_dec[rs]
            bd = jnp.exp(gc[(c + 1) * CHUNK - 1:(c + 1) * CHUNK, h:h + 1])
            s_cur = bd * s_cur + _dot_tn(jnp.concatenate(parts, axis=0), u_full)
        s_sc[h] = s_cur
        o = jnp.concatenate(outs, axis=0)
        o = _rms(o, gn) * _silu(pad_rows(z_all[:, sl]))
        y_ref[:, sl] = o[:t_real].astype(y_ref.dtype)

    @pl.when(t == nt - 1)
    def _():
        snew_ref[...] = s_sc[...]


def _gdn(aqkv, az, ab, conv_buf8, s0_arr, s0_index, lw, batch, seq):
    n_heads = lw['n_heads']
    t_real = min(ROWS, seq)
    nt = seq // t_real
    c = aqkv.shape[1]
    vw = az.shape[1]
    n_double = max(int(math.ceil(math.log2(min(CHUNK, seq)))) - 1, 0)
    body = functools.partial(_gdn_body, t_real=t_real, n_heads=n_heads, n_double=n_double)
    tok = lambda w: pl.BlockSpec((None, t_real, w), lambda b, t: (b, t, 0))
    s_shape = (n_heads, LANES, LANES)
    return pl.pallas_call(
        body,
        grid=(batch, nt),
        in_specs=[tok(c), tok(vw), tok(LANES),
                  pl.BlockSpec((None, 8, c), lambda b, t: (b, 0, 0)),
                  pl.BlockSpec((None,) * (s0_arr.ndim - 3) + s_shape, s0_index),
                  _resident(lw['conv_w'].shape), _resident((1, LANES)), _resident((1, LANES)),
                  _resident((1, LANES))],
        out_specs=[tok(vw), pl.BlockSpec((None,) + s_shape, lambda b, t: (b, 0, 0, 0))],
        out_shape=[jax.ShapeDtypeStruct((batch, seq, vw), BF16),
                   jax.ShapeDtypeStruct((batch,) + s_shape, F32)],
        scratch_shapes=[pltpu.VMEM((8, c), F32), pltpu.VMEM(s_shape, F32)],
        compiler_params=_cparams(("parallel", "arbitrary")),
        name="gdn",
    )(aqkv.reshape(batch, seq, c), az.reshape(batch, seq, vw), ab.reshape(batch, seq, LANES),
      conv_buf8, s0_arr, lw['conv_w'], lw['a_log'], lw['dt_bias'], lw['gdn_norm'])


def _gmlp_body(uv_ref, lng_ref, lnb_ref, ws_ref, bs_ref, y_ref, *vrows_ref, span):
    uv = uv_ref[...]
    half = uv.shape[1] // 2
    ge = 0.5 * uv * (1.0 + lax.erf(uv * (2.0 ** -0.5)))
    u = ge[:, :half]
    v = ge[:, half:]
    mu = jnp.mean(v, axis=-1, keepdims=True)
    vc = v - mu
    vn = vc * lax.rsqrt(jnp.mean(vc * vc, axis=-1, keepdims=True) + EPS) * lng_ref[...] + lnb_ref[...]
    if vrows_ref:
        vrows_ref[0][...] = vn
    row = lax.broadcasted_iota(jnp.int32, (ROWS, ROWS), 0)
    col = lax.broadcasted_iota(jnp.int32, (ROWS, ROWS), 1)
    mask = ((row & -span) == (col & -span)) & ((row & (span - 1) & -CHUNK) >= (col & (span - 1) & -CHUNK))
    for g in range(half // LANES):
        sl = slice(g * LANES, (g + 1) * LANES)
        w = jnp.where(mask, ws_ref[g], 0.0)
        s = _dot(w, vn[:, sl]) + bs_ref[:, sl]
        y_ref[:, sl] = (u[:, sl] * s).astype(y_ref.dtype)


def _gmlp(buv, lw, span, want_rows):
    ntok, w2 = buv.shape
    half = w2 // 2
    tok = lambda w: pl.BlockSpec((ROWS, w), lambda i: (i, 0))
    out_specs = [tok(half)]
    out_shape = [jax.ShapeDtypeStruct((ntok, half), BF16)]
    if want_rows:
        out_specs.append(tok(half))
        out_shape.append(jax.ShapeDtypeStruct((ntok, half), F32))
    ws, bs = (lw['ws_p'], lw['bs_p']) if span == ROWS else (lw['ws_s'], lw['bs_s'])
    return pl.pallas_call(
        functools.partial(_gmlp_body, span=span),
        grid=(ntok // ROWS,),
        in_specs=[tok(w2), _resident((1, half)), _resident((1, half)), _resident(ws.shape),
                  _resident(bs.shape)],
        out_specs=out_specs,
        out_shape=out_shape,
        compiler_params=_cparams(("parallel",)),
        name="gmlp",
    )(buv, lw['ln_v_g'], lw['ln_v_b'], ws, bs)


def _sb_block(q, kb, vb, carry, tri, mask):
    z = _dot_nt(q, kb) * (LANES ** -0.5)
    sp = _softplus(z)
    lf = -sp if mask is None else jnp.where(mask, -sp, 0.0)
    hi, lo = _split(lf)
    after = (jnp.dot(hi, tri, preferred_element_type=F32)
             + jnp.dot(lo, tri, preferred_element_type=F32))
    a = jnp.exp((z - sp) + after)
    if mask is not None:
        a = jnp.where(mask, a, 0.0)
    out = jnp.exp(carry) * _dot(a, vb)
    return out, jnp.sum(lf, axis=-1, keepdims=True)


def _sb_prompt_body(q_ref, k_ref, v_ref, o_ref, acc_sc, car_sc):
    i = pl.program_id(2)
    tq = q_ref.shape[0]
    q = q_ref[...].astype(BF16)
    row = lax.broadcasted_iota(jnp.int32, (tq, tq), 0)
    col = lax.broadcasted_iota(jnp.int32, (tq, tq), 1)
    tri = (row > col).astype(BF16)
    start = pl.multiple_of(i * tq, tq)
    out, csum = _sb_block(q, k_ref[pl.ds(start, tq), :], v_ref[pl.ds(start, tq), :],
                          jnp.zeros((tq, 1), F32), tri, col < row)
    acc_sc[...] = out
    car_sc[...] = csum

    def cond(st):
        j, live = st
        return (j >= 0) & (live > SB_DEAD_LOG)

    def body(st):
        j, _ = st
        off = pl.multiple_of(j * tq, tq)
        carry = car_sc[...]
        out, csum = _sb_block(q, k_ref[pl.ds(off, tq), :], v_ref[pl.ds(off, tq), :], carry, tri, None)
        acc_sc[...] += out
        carry = carry + csum
        car_sc[...] = carry
        return j - 1, jnp.max(carry)

    lax.while_loop(cond, body, (i - 1, jnp.max(csum)))
    o_ref[...] = acc_sc[...].astype(o_ref.dtype)


def _sb_prompt(q, k, v, batch, seq, n_heads):
    tq = min(SB_BLOCK, seq)
    w = n_heads * LANES
    blk = pl.BlockSpec((None, tq, LANES), lambda b, h, i: (b, i, h))
    full = pl.BlockSpec((None, seq, LANES), lambda b, h, i: (b, 0, h))
    return pl.pallas_call(
        _sb_prompt_body,
        grid=(batch, n_heads, seq // tq),
        in_specs=[blk, full, full],
        out_specs=blk,
        out_shape=jax.ShapeDtypeStruct((batch, seq, w), BF16),
        scratch_shapes=[pltpu.VMEM((tq, LANES), F32), pltpu.VMEM((tq, 1), F32)],
        compiler_params=_cparams(("parallel", "parallel", "arbitrary")),
        name="sb_prompt",
    )(q.reshape(batch, seq, w), k.reshape(batch, seq, w), v.reshape(batch, seq, w))


def _sb_sample_body(q_ref, k_ref, v_ref, kp_ref, vp_ref, o_ref, acc_sc, car_sc, *, n_heads, tk):
    tq = q_ref.shape[0]
    past = kp_ref.shape[0]
    row = lax.broadcasted_iota(jnp.int32, (tq, LANES), 0)
    col = lax.broadcasted_iota(jnp.int32, (tq, LANES), 1)
    r2 = lax.broadcasted_iota(jnp.int32, (LANES, LANES), 0)
    c2 = lax.broadcasted_iota(jnp.int32, (LANES, LANES), 1)
    tri_new = (r2 > c2).astype(BF16)
    r3 = lax.broadcasted_iota(jnp.int32, (tk, tk), 0)
    c3 = lax.broadcasted_iota(jnp.int32, (tk, tk), 1)
    tri_past = (r3 > c3).astype(BF16)
    pad = jnp.zeros((LANES - tq, LANES), F32)
    for h in range(n_heads):
        sl = slice(h * LANES, (h + 1) * LANES)
        q = q_ref[:, sl].astype(BF16)
        kn = jnp.concatenate([k_ref[:, sl], pad], axis=0)
        vn = jnp.concatenate([v_ref[:, sl], pad], axis=0)
        out, csum = _sb_block(q, kn, vn, jnp.zeros((tq, 1), F32), tri_new, col < row)
        acc_sc[...] = out
        car_sc[...] = csum

        def cond(st):
            j, live = st
            return (j >= 0) & (live > SB_DEAD_LOG)

        def body(st, sl=sl, q=q):
            j, _ = st
            off = pl.multiple_of(j * tk, tk)
            carry = car_sc[...]
            out, csum = _sb_block(q, kp_ref[pl.ds(off, tk), sl], vp_ref[pl.ds(off, tk), sl],
                                  carry, tri_past, None)
            acc_sc[...] += out
            carry = carry + csum
            car_sc[...] = carry
            return j - 1, jnp.max(carry)

        lax.while_loop(cond, body, (past // tk - 1, jnp.max(csum)))
        o_ref[:, sl] = acc_sc[...].astype(o_ref.dtype)


def _sb_sample(q, k, v, cache_k, cache_v, layer, batch, seq, n_heads):
    w = n_heads * LANES
    past = cache_k.shape[2]
    tk = min(SB_BLOCK, past)
    new = pl.BlockSpec((None, seq, w), lambda b: (b, 0, 0))
    old = pl.BlockSpec((None, None, past, w), lambda b: (layer, b, 0, 0))
    return pl.pallas_call(
        functools.partial(_sb_sample_body, n_heads=n_heads, tk=tk),
        grid=(batch,),
        in_specs=[new, new, new, old, old],
        out_specs=new,
        out_shape=jax.ShapeDtypeStruct((batch, seq, w), BF16),
        scratch_shapes=[pltpu.VMEM((seq, LANES), F32), pltpu.VMEM((seq, 1), F32)],
        compiler_params=_cparams(("parallel",)),
        name="sb_sample",
    )(q.reshape(batch, seq, w), k.reshape(batch, seq, w), v.reshape(batch, seq, w),
      cache_k.reshape(cache_k.shape[0], batch, past, w), cache_v.reshape(cache_v.shape[0], batch, past, w))


def _post_body(h_ref, gate_ref, ya_ref, yb_ref, yc_ref, p_ref,
               wpa_ref, wpb_ref, wpc_ref, wo_ref, nf_ref, wf1_ref, wf3_ref, wf2_ref,
               npl_ref, wpg_ref, wpp_ref, nfin_ref, o_ref, *, final, n_ffn_chunks):
    d = functools.partial(jnp.dot, preferred_element_type=F32)
    h = h_ref[...]
    dm = h.shape[1]
    gates = jax.nn.sigmoid(gate_ref[...])
    merged = (gates[:, :dm] * d(ya_ref[...], wpa_ref[...])
              + gates[:, dm:2 * dm] * d(yb_ref[...], wpb_ref[...])
              + gates[:, 2 * dm:] * d(yc_ref[...], wpc_ref[...]))
    h = h + d(merged.astype(BF16), wo_ref[...])
    xn = _rms(h, nf_ref[...]).astype(BF16)
    hid = wf1_ref.shape[1]
    cw = hid // n_ffn_chunks
    ff = None
    for c in range(n_ffn_chunks):
        cs = slice(c * cw, (c + 1) * cw)
        a1 = d(xn, wf1_ref[:, cs])
        a3 = d(xn, wf3_ref[:, cs])
        part = d((_silu(a1) * a3).astype(BF16), wf2_ref[cs, :])
        ff = part if ff is None else ff + part
    h = h + ff
    xn = _rms(h, npl_ref[...]).astype(BF16)
    h = h + jax.nn.sigmoid(d(xn, wpg_ref[...])) * d(p_ref[...].astype(BF16), wpp_ref[...])
    if final:
        h = _rms(h, nfin_ref[...])
    o_ref[...] = h


def _post(h, gate, ya, yb, yc, p_arr, layer, lw, norm_final, final, tm):
    ntok, dm = h.shape
    tm = min(tm, ntok)
    bw = ya.shape[1]
    pd = p_arr.shape[-1]
    p3 = p_arr.reshape(p_arr.shape[0], ntok, pd)
    tok = lambda w: pl.BlockSpec((tm, w), lambda i: (i, 0))
    weights = [lw['w_pa'], lw['w_pb'], lw['w_pc'], lw['w_o'], lw['norm_ffn'], lw['w_f1'], lw['w_f3'],
               lw['w_f2'], lw['norm_ple'], lw['w_pg'], lw['w_pp'], norm_final]
    hid = lw['w_f1'].shape[1]
    n_chunks = next(n for n in (2, 1) if hid % (n * LANES) == 0)
    return pl.pallas_call(
        functools.partial(_post_body, final=final, n_ffn_chunks=n_chunks),
        grid=(ntok // tm,),
        in_specs=[tok(dm), tok(gate.shape[1]), tok(bw), tok(bw), tok(bw),
                  pl.BlockSpec((None, tm, pd), lambda i: (layer, i, 0))]
                 + [_resident(w.shape) for w in weights],
        out_specs=tok(dm),
        out_shape=jax.ShapeDtypeStruct((ntok, dm), F32),
        compiler_params=_cparams(("parallel",)),
        name="post",
    )(h, gate, ya, yb, yc, p3, *weights)


def _layer_weights(i, a_log, dt_bias, w_in, conv_w, norm_mix, gdn_norm, w_pa, ln_v_g, ln_v_b, w_s, b_s,
                   w_pb, w_pc, w_o, norm_ffn, w_ffn_in, w_ffn_out, norm_ple, w_ple_gate, w_ple_proj,
                   dec_seq):
    n_heads = a_log.shape[1]
    c = conv_w.shape[2]
    vw = (c - 2 * n_heads * LANES)
    n_uv = 2 * w_s.shape[1] * LANES
    dm = w_in.shape[1]
    n_sb = w_pc.shape[1]
    o = 0
    w = w_in[i]
    w_a = w[:, o:o + c + vw]; o += c + vw
    w_ab = w[:, o:o + 2 * n_heads]; o += 2 * n_heads
    w_b = w[:, o:o + n_uv]; o += n_uv
    w_c = w[:, o:o + 3 * n_sb]; o += 3 * n_sb
    w_g = w[:, o:]
    row = lambda x: x.reshape(1, -1).astype(F32)
    lane_pad = lambda x: jnp.pad(x.reshape(1, -1), ((0, 0), (0, LANES - x.size))).astype(F32)
    hid = w_ffn_out.shape[1]
    groups, span = w_s.shape[1], w_s.shape[2]
    rep = ROWS // dec_seq
    lw = dict(
        n_heads=n_heads, n_qkv=c, n_z=vw, n_uv=n_uv, n_sb=n_sb, n_gate=w_g.shape[1],
        norm_mix=row(norm_mix[i]),
        w_a=w_a.astype(BF16), w_ab=jnp.pad(w_ab, ((0, 0), (0, LANES - 2 * n_heads))).astype(BF16),
        w_b=w_b.astype(BF16), w_c=w_c.astype(BF16), w_g=w_g.astype(BF16),
        conv_w=conv_w[i].astype(F32), a_log=lane_pad(a_log[i]), dt_bias=lane_pad(dt_bias[i]),
        gdn_norm=row(gdn_norm[i]),
        ln_v_g=row(ln_v_g[i]), ln_v_b=row(ln_v_b[i]),
        ws_p=w_s[i].astype(F32),
        bs_p=jnp.repeat(b_s[i].T, LANES, axis=1).astype(F32),
        ws_s=jnp.tile(w_s[i][:, :dec_seq, :dec_seq], (1, rep, rep)).astype(F32),
        bs_s=jnp.tile(jnp.repeat(b_s[i][:, :dec_seq].T, LANES, axis=1), (rep, 1)).astype(F32),
        w_pa=w_pa[i].astype(BF16), w_pb=w_pb[i].astype(BF16), w_pc=w_pc[i].astype(BF16),
        w_o=w_o[i].astype(BF16), norm_ffn=row(norm_ffn[i]),
        w_f1=w_ffn_in[i][:, :w_ffn_in.shape[2] // 2].astype(BF16),
        w_f3=w_ffn_in[i][:, w_ffn_in.shape[2] // 2:].astype(BF16),
        w_f2=w_ffn_out[i].astype(BF16), norm_ple=row(norm_ple[i]),
        w_pg=w_ple_gate[i].astype(BF16), w_pp=w_ple_proj[i].astype(BF16),
    )
    assert span == ROWS and hid == lw['w_f1'].shape[1]
    return lw


def _layer(h, p_arr, layer, conv_buf8, s0_arr, s0_index, caches, lw, norm_final, final, batch, seq, tm):
    n_heads = lw['n_heads']
    aqkv, az, ab, buv, q, k, v, gate = _in_proj(h, lw['norm_mix'], lw, tm)
    ya, s_new = _gdn(aqkv, az, ab, conv_buf8, s0_arr, s0_index, lw, batch, seq)
    conv_new = aqkv.reshape(batch, seq, -1)[:, seq - 3:, :]
    if caches is None:
        (yb,) = _gmlp(buv, lw, ROWS, False)
        v_rows = None
        yc = _sb_prompt(q, k, v, batch, seq, n_heads)
    else:
        yb, v_rows = _gmlp(buv, lw, seq, True)
        yc = _sb_sample(q, k, v, caches[0], caches[1], layer, batch, seq, n_heads)
    bw = ya.shape[-1]
    h = _post(h, gate, ya.reshape(-1, bw), yb, yc.reshape(-1, bw), p_arr, layer, lw, norm_final, final, tm)
    kv_shape = (batch, seq, n_heads, LANES)
    return h, conv_new, s_new, k.reshape(kv_shape), v.reshape(kv_shape), v_rows


def kernel(x_prompt, x_sample, state_gdn_conv, state_gdn_s, cache_sb_k, cache_sb_v, p_prompt, p_sample, norm_mix, w_in, conv_w, a_log, dt_bias, gdn_norm, w_pa, ln_v_g, ln_v_b, w_s, b_s, w_pb, w_pc, w_o, norm_ffn, w_ffn_in, w_ffn_out, norm_ple, w_ple_gate, w_ple_proj, norm_final):
    depth = w_in.shape[0]
    bp, sp, dm = x_prompt.shape
    bs, ss, _ = x_sample.shape
    n_heads = a_log.shape[1]
    hp = x_prompt.reshape(bp * sp, dm)
    hs = x_sample.reshape(bs * ss, dm)
    nfin = norm_final.reshape(1, dm).astype(F32)
    c = conv_w.shape[2]
    zero_buf = jnp.zeros((bp, 8, c), F32)
    zero_s = jnp.zeros((bp, n_heads, LANES, LANES), F32)
    outs = [[] for _ in range(9)]
    for i in range(depth):
        lw = _layer_weights(i, a_log, dt_bias, w_in, conv_w, norm_mix, gdn_norm, w_pa, ln_v_g, ln_v_b, w_s,
                            b_s, w_pb, w_pc, w_o, norm_ffn, w_ffn_in, w_ffn_out, norm_ple, w_ple_gate,
                            w_ple_proj, ss)
        final = i == depth - 1
        hp, pc, ps, pk, pv, _ = _layer(hp, p_prompt, i, zero_buf, zero_s, lambda b, t: (b, 0, 0, 0), None,
                                       lw, nfin, final, bp, sp, 256)
        buf8 = jnp.pad(state_gdn_conv[i], ((0, 0), (8 - state_gdn_conv.shape[2], 0), (0, 0)))
        hs, sc, sn, sk, sv, sm = _layer(hs, p_sample, i, buf8, state_gdn_s,
                                        lambda b, t, i=i: (i, b, 0, 0, 0), (cache_sb_k, cache_sb_v),
                                        lw, nfin, final, bs, ss, 256)
        for lst, val in zip(outs, (pc, ps, pk, pv, sc, sn, sk, sv, sm.reshape(bs, ss, -1))):
            lst.append(val)
    stacked = [jnp.stack(l) for l in outs]
    return (hp.reshape(bp, sp, dm), hs.reshape(bs, ss, dm), *stacked)
```

```python
import functools
import math

import jax
import jax.numpy as jnp
from jax import lax
from jax.experimental import pallas as pl
from jax.experimental.pallas import tpu as pltpu

F32 = jnp.float32
BF16 = jnp.bfloat16

EPS = 1e-6
CHUNK = 64
LANES = 128
ROWS = 128
GDN_STREAMS = 2
SB_BLOCK = 256
SB_DEAD_LOG = -104.0
VMEM_LIMIT = 56 * 1024 * 1024


def _cparams(sem):
    return pltpu.CompilerParams(dimension_semantics=sem, vmem_limit_bytes=VMEM_LIMIT)


def _resident(shape):
    nd = len(shape)
    return pl.BlockSpec(shape, lambda *_: (0,) * nd, pipeline_mode=pl.Buffered(1))


def _rms(x, g):
    return x * lax.rsqrt(jnp.mean(x * x, axis=-1, keepdims=True) + EPS) * g


def _dot(a, b):
    return jnp.dot(a.astype(BF16), b.astype(BF16), preferred_element_type=F32)


def _dot_nt(a, b):
    return lax.dot_general(a.astype(BF16), b.astype(BF16), (((1,), (1,)), ((), ())),
                           preferred_element_type=F32)


def _dot_tn(a, b):
    return lax.dot_general(a.astype(BF16), b.astype(BF16), (((0,), (0,)), ((), ())),
                           preferred_element_type=F32)


def _split(x):
    hi = x.astype(BF16)
    lo = (x - hi.astype(F32)).astype(BF16)
    return hi, lo


def _blockdiag(x):
    z = jnp.zeros((x.shape[0], LANES), x.dtype)
    return jnp.concatenate([jnp.concatenate([x[:, :LANES], z], axis=1),
                            jnp.concatenate([z, x[:, LANES:]], axis=1)], axis=0)


def _pdot(a, b):
    return jnp.dot(a.astype(BF16), _blockdiag(b.astype(BF16)), preferred_element_type=F32)


def _pdot3(a, b):
    ah, al = _split(a)
    bh, bl = _split(b)
    bdh, bdl = _blockdiag(bh), _blockdiag(bl)
    d = functools.partial(jnp.dot, preferred_element_type=F32)
    return d(ah, bdh) + (d(al, bdh) + d(ah, bdl))


def _softplus(x):
    return jnp.maximum(x, 0.0) + jnp.log(1.0 + jnp.exp(-jnp.abs(x)))


def _silu(x):
    return x * jax.nn.sigmoid(x)


def _in_proj_body(h_ref, g_ref, wa_ref, wab_ref, wb_ref, wc_ref, wg_ref,
                  aqkv_ref, az_ref, ab_ref, buv_ref, qb_ref, kb_ref, vb_ref, k4_ref, v4_ref, gate_ref):
    xn = _rms(h_ref[...], g_ref[...]).astype(BF16)
    d = functools.partial(jnp.dot, preferred_element_type=F32)
    tm = h_ref.shape[0]
    n_qkv = aqkv_ref.shape[1]
    r = d(xn, wa_ref[...])
    aqkv_ref[...] = r[:, :n_qkv]
    az_ref[...] = r[:, n_qkv:]
    ab_ref[...] = d(xn, wab_ref[...])
    buv_ref[...] = d(xn, wb_ref[...])
    r = d(xn, wc_ref[...])
    w = qb_ref.shape[1]
    n_heads = w // LANES
    qb_ref[...] = r[:, :w].astype(BF16)
    kb_ref[...] = r[:, w:2 * w].astype(BF16)
    vb_ref[...] = r[:, 2 * w:].astype(BF16)
    for h in range(n_heads):
        k4_ref[pl.ds(h, tm, stride=n_heads), :] = r[:, w + h * LANES:w + (h + 1) * LANES]
        v4_ref[pl.ds(h, tm, stride=n_heads), :] = r[:, 2 * w + h * LANES:2 * w + (h + 1) * LANES]
    gate_ref[...] = d(xn, wg_ref[...])


def _in_proj(h, norm_g, lw, tm):
    ntok, dm = h.shape
    tm = min(tm, ntok)
    n_heads = lw['n_sb'] // LANES
    tok = lambda w, dt=F32: (pl.BlockSpec((tm, w), lambda i: (i, 0)), jax.ShapeDtypeStruct((ntok, w), dt))
    kv4 = (pl.BlockSpec((tm * n_heads, LANES), lambda i: (i, 0)),
           jax.ShapeDtypeStruct((ntok * n_heads, LANES), F32))
    outs = [tok(lw['n_qkv']), tok(lw['n_z']), tok(LANES), tok(lw['n_uv']),
            tok(lw['n_sb'], BF16), tok(lw['n_sb'], BF16), tok(lw['n_sb'], BF16), kv4, kv4, tok(lw['n_gate'])]
    return pl.pallas_call(
        _in_proj_body,
        grid=(ntok // tm,),
        in_specs=[tok(dm)[0], _resident((1, dm)), _resident(lw['w_a'].shape), _resident(lw['w_ab'].shape),
                  _resident(lw['w_b'].shape), _resident(lw['w_c'].shape), _resident(lw['w_g'].shape)],
        out_specs=[o[0] for o in outs],
        out_shape=[o[1] for o in outs],
        compiler_params=_cparams(("parallel",)),
        name="in_proj",
    )(h, norm_g, lw['w_a'], lw['w_ab'], lw['w_b'], lw['w_c'], lw['w_g'])


def _gdn_stream(x, prev, ab, z_all, s_pairs, cw, alog, dtb, gn, *, t_real, n_heads, n_double):
    R = ROWS
    L2 = 2 * LANES
    qk_w = n_heads * LANES
    row = lax.broadcasted_iota(jnp.int32, (R, LANES), 0)
    valid = row < t_real
    row2 = lax.broadcasted_iota(jnp.int32, (R, L2), 0)
    col2 = lax.broadcasted_iota(jnp.int32, (R, L2), 1) & (LANES - 1)

    n_tap = cw.shape[0]
    row8 = lax.broadcasted_iota(jnp.int32, (8, x.shape[1]), 0)
    y = x * cw[n_tap - 1:n_tap, :]
    for sh in range(1, n_tap):
        xs = pltpu.roll(x, sh, 0)
        head = jnp.where(row8 < sh, pltpu.roll(prev, sh, 0), xs[:8])
        xs = jnp.concatenate([head, xs[8:]], axis=0)
        y = y + xs * cw[n_tap - 1 - sh:n_tap - sh, :]
    y = _silu(y)

    g = -jnp.exp(alog) * _softplus(ab + dtb)
    beta = jax.nn.sigmoid(ab)
    if t_real != R:
        g = jnp.where(valid, g, 0.0)
        beta = jnp.where(valid, beta, 0.0)
    rm = row & (CHUNK - 1)
    gc = g
    sh = 1
    while sh < CHUNK:
        gc = gc + jnp.where(rm >= sh, pltpu.roll(gc, sh, 0), 0.0)
        sh *= 2
    gc_t = gc.T
    e_gc = jnp.exp(gc)
    g_last = jnp.where(row < CHUNK, gc[CHUNK - 1:CHUNK, :], gc[R - 1:R, :])
    e_rest = jnp.exp(g_last - gc)

    same_blk = (row2 & -CHUNK) == (col2 & -CHUNK)
    incl = same_blk & (row2 >= col2)
    strict = same_blk & (row2 > col2)
    eye = (row2 == col2).astype(F32)
    zeros_half = jnp.zeros((CHUNK, L2), F32)

    def cols(m, h0):
        return jnp.concatenate([jnp.broadcast_to(m[:, h0:h0 + 1], (R, LANES)),
                                jnp.broadcast_to(m[:, h0 + 1:h0 + 2], (R, LANES))], axis=1)

    def l2n(m):
        return m * lax.rsqrt(jnp.sum(m * m, axis=-1, keepdims=True) + EPS)

    def chain(p):
        h0 = 2 * p
        ps = slice(h0 * LANES, (h0 + 2) * LANES)
        q0, q1 = y[:, h0 * LANES:(h0 + 1) * LANES], y[:, (h0 + 1) * LANES:(h0 + 2) * LANES]
        k0 = y[:, qk_w + h0 * LANES:qk_w + (h0 + 1) * LANES]
        k1 = y[:, qk_w + (h0 + 1) * LANES:qk_w + (h0 + 2) * LANES]
        qp = jnp.concatenate([l2n(q0), l2n(q1)], axis=1) * (LANES ** -0.5)
        kp = jnp.concatenate([l2n(k0), l2n(k1)], axis=1)
        vp = y[:, 2 * qk_w + h0 * LANES:2 * qk_w + (h0 + 2) * LANES]
        if t_real != R:
            ok = row2 < t_real
            qp, kp, vp = (jnp.where(ok, m, 0.0) for m in (qp, kp, vp))

        gcol = cols(gc, h0)
        grow = jnp.concatenate([gc_t[h0:h0 + 1, :], gc_t[h0 + 1:h0 + 2, :]], axis=1)
        bcol = cols(beta, n_heads + h0)
        ecol = cols(e_gc, h0)
        rcol = cols(e_rest, h0)

        decay = jnp.where(incl, jnp.exp(gcol - grow), 0.0)
        kq = _dot_nt(jnp.concatenate([kp, qp], axis=0), _blockdiag(kp))
        yield
        a_neg = jnp.where(strict, -(bcol * decay * kq[:R]), 0.0)
        p_mat = decay * kq[R:]

        t_inv = eye + a_neg
        pw = _pdot(a_neg, a_neg) if n_double else None
        yield
        for it in range(n_double):
            if it < n_double - 1:
                both = _pdot(jnp.concatenate([pw, t_inv], axis=0), pw)
                pw = both[:R]
                t_inv = t_inv + both[R:]
            else:
                t_inv = t_inv + _pdot(t_inv, pw)
            yield
        resid = (eye - t_inv) + _pdot3(a_neg, t_inv)
        yield
        t_inv = t_inv + _pdot(t_inv, resid)
        yield

        k_in = ecol * kp
        q_dec = ecol * qp
        k_dec = rcol * kp
        s_cur = s_pairs[p]
        outs = []
        for c in range(R // CHUNK):
            rs = slice(c * CHUNK, (c + 1) * CHUNK)
            ks_qs = _pdot(jnp.concatenate([k_in[rs], q_dec[rs]], axis=0), s_cur)
            yield
            rhs = bcol[rs] * (vp[rs] - ks_qs[:CHUNK])
            parts = [zeros_half] * (R // CHUNK)
            parts[c] = rhs
            u = _pdot3(t_inv[rs, :], jnp.concatenate(parts, axis=0))
            yield
            parts[c] = u
            u_full = jnp.concatenate(parts, axis=0)
            outs.append(ks_qs[CHUNK:] + _pdot(p_mat[rs, :], u_full))
            parts[c] = k_dec[rs]
            kd = jnp.concatenate(parts, axis=0)
            kd = jnp.concatenate([kd[:, :LANES], kd[:, LANES:]], axis=0)
            last = (c + 1) * CHUNK - 1
            bd = jnp.exp(jnp.concatenate([jnp.broadcast_to(gc[last:last + 1, h0:h0 + 1], (1, LANES)),
                                          jnp.broadcast_to(gc[last:last + 1, h0 + 1:h0 + 2], (1, LANES))],
                                         axis=1))
            s_cur = bd * s_cur + _dot_tn(kd, _blockdiag(u_full.astype(BF16)))
            yield
        o = jnp.concatenate(outs, axis=0)
        zg = _silu(z_all[:, ps])
        return jnp.concatenate([_rms(o[:, :LANES], gn), _rms(o[:, LANES:], gn)], axis=1) * zg, s_cur

    return [chain(p) for p in range(n_heads // 2)]


def _lockstep(gens):
    done = [None] * len(gens)
    live = list(range(len(gens)))
    while live:
        for i in list(live):
            try:
                next(gens[i])
            except StopIteration as stop:
                done[i] = stop.value
                live.remove(i)
    return done


def _gdn_body(qkv_ref, z_ref, ab_ref, cbuf_ref, s0_ref, cw_ref, alog_ref, dtb_ref, gn_ref,
              y_ref, snew_ref, xprev_sc, s_sc, *, t_real, n_heads, n_double):
    t = pl.program_id(1)
    nt = pl.num_programs(1)
    n_streams = qkv_ref.shape[0]
    n_pairs = n_heads // 2

    @pl.when(t == 0)
    def _():
        xprev_sc[...] = cbuf_ref[...]
        for s in range(n_streams):
            for p in range(n_pairs):
                s_sc[s, p] = jnp.concatenate([s0_ref[s, 2 * p], s0_ref[s, 2 * p + 1]], axis=1)

    def pad_rows(x):
        if t_real == ROWS:
            return x
        return jnp.concatenate([x, jnp.zeros((ROWS - t_real, x.shape[1]), x.dtype)], axis=0)

    chains = []
    for s in range(n_streams):
        x = pad_rows(qkv_ref[s])
        chains += _gdn_stream(x, xprev_sc[s], pad_rows(ab_ref[s]), pad_rows(z_ref[s]),
                              [s_sc[s, p] for p in range(n_pairs)], cw_ref[...], alog_ref[...],
                              dtb_ref[...], gn_ref[...], t_real=t_real, n_heads=n_heads, n_double=n_double)
        xprev_sc[s] = x[ROWS - 8:, :]
    done = _lockstep(chains)
    for s in range(n_streams):
        for p in range(n_pairs):
            y, s_new = done[s * n_pairs + p]
            s_sc[s, p] = s_new
            y_ref[s, :, 2 * p * LANES:2 * (p + 1) * LANES] = y[:t_real].astype(y_ref.dtype)

    @pl.when(t == nt - 1)
    def _():
        for s in range(n_streams):
            for p in range(n_pairs):
                sp = s_sc[s, p]
                snew_ref[s, 2 * p] = sp[:, :LANES]
                snew_ref[s, 2 * p + 1] = sp[:, LANES:]


def _gdn(aqkv, az, ab, conv_buf8, s0_arr, s0_index, lw, batch, seq):
    n_heads = lw['n_heads']
    ns = GDN_STREAMS
    assert batch % ns == 0 and n_heads % 2 == 0
    t_real = min(ROWS, seq)
    nt = seq // t_real
    c = aqkv.shape[1]
    vw = az.shape[1]
    n_double = max(int(math.ceil(math.log2(min(CHUNK, seq)))) - 1, 0)
    body = functools.partial(_gdn_body, t_real=t_real, n_heads=n_heads, n_double=n_double)
    tok = lambda w: pl.BlockSpec((ns, t_real, w), lambda b, t: (b, t, 0))
    s_shape = (ns, n_heads, LANES, LANES)
    return pl.pallas_call(
        body,
        grid=(batch // ns, nt),
        in_specs=[tok(c), tok(vw), tok(LANES),
                  pl.BlockSpec((ns, 8, c), lambda b, t: (b, 0, 0)),
                  pl.BlockSpec((None,) * (s0_arr.ndim - 4) + s_shape, s0_index),
                  _resident(lw['conv_w'].shape), _resident((1, LANES)), _resident((1, LANES)),
                  _resident((1, LANES))],
        out_specs=[tok(vw), pl.BlockSpec(s_shape, lambda b, t: (b, 0, 0, 0))],
        out_shape=[jax.ShapeDtypeStruct((batch, seq, vw), BF16),
                   jax.ShapeDtypeStruct((batch,) + s_shape[1:], F32)],
        scratch_shapes=[pltpu.VMEM((ns, 8, c), F32), pltpu.VMEM((ns, n_heads // 2, LANES, 2 * LANES), F32)],
        compiler_params=_cparams(("parallel", "arbitrary")),
        name="gdn",
    )(aqkv.reshape(batch, seq, c), az.reshape(batch, seq, vw), ab.reshape(batch, seq, LANES),
      conv_buf8, s0_arr, lw['conv_w'], lw['a_log'], lw['dt_bias'], lw['gdn_norm'])


def _gmlp_body(uv_ref, lng_ref, lnb_ref, ws_ref, bs_ref, y_ref, *vrows_ref, span):
    uv = uv_ref[...]
    half = uv.shape[1] // 2
    ge = 0.5 * uv * (1.0 + lax.erf(uv * (2.0 ** -0.5)))
    u = ge[:, :half]
    v = ge[:, half:]
    mu = jnp.mean(v, axis=-1, keepdims=True)
    vc = v - mu
    vn = vc * lax.rsqrt(jnp.mean(vc * vc, axis=-1, keepdims=True) + EPS) * lng_ref[...] + lnb_ref[...]
    if vrows_ref:
        vrows_ref[0][...] = vn
    row = lax.broadcasted_iota(jnp.int32, (ROWS, ROWS), 0)
    col = lax.broadcasted_iota(jnp.int32, (ROWS, ROWS), 1)
    mask = ((row & -span) == (col & -span)) & ((row & (span - 1) & -CHUNK) >= (col & (span - 1) & -CHUNK))
    for g in range(half // LANES):
        sl = slice(g * LANES, (g + 1) * LANES)
        w = jnp.where(mask, ws_ref[g], 0.0)
        s = _dot(w, vn[:, sl]) + bs_ref[:, sl]
        y_ref[:, sl] = (u[:, sl] * s).astype(y_ref.dtype)


def _gmlp(buv, lw, span, want_rows):
    ntok, w2 = buv.shape
    half = w2 // 2
    tok = lambda w: pl.BlockSpec((ROWS, w), lambda i: (i, 0))
    out_specs = [tok(half)]
    out_shape = [jax.ShapeDtypeStruct((ntok, half), BF16)]
    if want_rows:
        out_specs.append(tok(half))
        out_shape.append(jax.ShapeDtypeStruct((ntok, half), F32))
    ws, bs = (lw['ws_p'], lw['bs_p']) if span == ROWS else (lw['ws_s'], lw['bs_s'])
    return pl.pallas_call(
        functools.partial(_gmlp_body, span=span),
        grid=(ntok // ROWS,),
        in_specs=[tok(w2), _resident((1, half)), _resident((1, half)), _resident(ws.shape),
                  _resident(bs.shape)],
        out_specs=out_specs,
        out_shape=out_shape,
        compiler_params=_cparams(("parallel",)),
        name="gmlp",
    )(buv, lw['ln_v_g'], lw['ln_v_b'], ws, bs)


def _sb_block(q, kb, vb, carry, tri, mask):
    z = _dot_nt(q, kb) * (LANES ** -0.5)
    sp = _softplus(z)
    lf = -sp if mask is None else jnp.where(mask, -sp, 0.0)
    hi, lo = _split(lf)
    after = (jnp.dot(hi, tri, preferred_element_type=F32)
             + jnp.dot(lo, tri, preferred_element_type=F32))
    a = jnp.exp((z - sp) + after)
    if mask is not None:
        a = jnp.where(mask, a, 0.0)
    out = jnp.exp(carry) * _dot(a, vb)
    return out, jnp.sum(lf, axis=-1, keepdims=True)


def _sb_prompt_body(q_ref, k_ref, v_ref, o_ref, acc_sc, car_sc):
    i = pl.program_id(2)
    tq = q_ref.shape[0]
    q = q_ref[...]
    row = lax.broadcasted_iota(jnp.int32, (tq, tq), 0)
    col = lax.broadcasted_iota(jnp.int32, (tq, tq), 1)
    tri = (row > col).astype(BF16)
    start = pl.multiple_of(i * tq, tq)
    out, csum = _sb_block(q, k_ref[pl.ds(start, tq), :], v_ref[pl.ds(start, tq), :],
                          jnp.zeros((tq, 1), F32), tri, col < row)
    acc_sc[...] = out
    car_sc[...] = csum

    def cond(st):
        j, live = st
        return (j >= 0) & (live > SB_DEAD_LOG)

    def body(st):
        j, _ = st
        off = pl.multiple_of(j * tq, tq)
        carry = car_sc[...]
        out, csum = _sb_block(q, k_ref[pl.ds(off, tq), :], v_ref[pl.ds(off, tq), :], carry, tri, None)
        acc_sc[...] += out
        carry = carry + csum
        car_sc[...] = carry
        return j - 1, jnp.max(carry)

    lax.while_loop(cond, body, (i - 1, jnp.max(csum)))
    o_ref[...] = acc_sc[...].astype(o_ref.dtype)


def _sb_prompt(q, k, v, batch, seq, n_heads):
    tq = min(SB_BLOCK, seq)
    w = n_heads * LANES
    blk = pl.BlockSpec((None, tq, LANES), lambda b, h, i: (b, i, h))
    full = pl.BlockSpec((None, seq, LANES), lambda b, h, i: (b, 0, h))
    return pl.pallas_call(
        _sb_prompt_body,
        grid=(batch, n_heads, seq // tq),
        in_specs=[blk, full, full],
        out_specs=blk,
        out_shape=jax.ShapeDtypeStruct((batch, seq, w), BF16),
        scratch_shapes=[pltpu.VMEM((tq, LANES), F32), pltpu.VMEM((tq, 1), F32)],
        compiler_params=_cparams(("parallel", "parallel", "arbitrary")),
        name="sb_prompt",
    )(q.reshape(batch, seq, w), k.reshape(batch, seq, w), v.reshape(batch, seq, w))


def _sb_sample_body(q_ref, k_ref, v_ref, kp_ref, vp_ref, o_ref, acc_sc, car_sc, *, n_heads, tk):
    tq = q_ref.shape[0]
    past = kp_ref.shape[0] // n_heads
    row = lax.broadcasted_iota(jnp.int32, (tq, LANES), 0)
    col = lax.broadcasted_iota(jnp.int32, (tq, LANES), 1)
    r2 = lax.broadcasted_iota(jnp.int32, (LANES, LANES), 0)
    c2 = lax.broadcasted_iota(jnp.int32, (LANES, LANES), 1)
    tri_new = (r2 > c2).astype(BF16)
    r3 = lax.broadcasted_iota(jnp.int32, (tk, tk), 0)
    c3 = lax.broadcasted_iota(jnp.int32, (tk, tk), 1)
    tri_past = (r3 > c3).astype(BF16)
    pad = jnp.zeros((LANES - tq, LANES), BF16)
    for h in range(n_heads):
        sl = slice(h * LANES, (h + 1) * LANES)
        q = q_ref[:, sl]
        kn = jnp.concatenate([k_ref[:, sl], pad], axis=0)
        vn = jnp.concatenate([v_ref[:, sl], pad], axis=0)
        out, csum = _sb_block(q, kn, vn, jnp.zeros((tq, 1), F32), tri_new, col < row)
        acc_sc[...] = out
        car_sc[...] = csum

        def cond(st):
            j, live = st
            return (j >= 0) & (live > SB_DEAD_LOG)

        def body(st, h=h, q=q):
            j, _ = st
            rows = pl.ds(j * (tk * n_heads) + h, tk, stride=n_heads)
            carry = car_sc[...]
            out, csum = _sb_block(q, kp_ref[rows, :], vp_ref[rows, :], carry, tri_past, None)
            acc_sc[...] += out
            carry = carry + csum
            car_sc[...] = carry
            return j - 1, jnp.max(carry)

        lax.while_loop(cond, body, (past // tk - 1, jnp.max(csum)))
        o_ref[:, sl] = acc_sc[...].astype(o_ref.dtype)


def _sb_sample(q, k, v, cache_k, cache_v, layer, batch, seq, n_heads):
    w = n_heads * LANES
    depth, _, past = cache_k.shape[:3]
    tk = min(SB_BLOCK, past)
    new = pl.BlockSpec((None, seq, w), lambda b: (b, 0, 0))
    old = pl.BlockSpec((None, None, past * n_heads, LANES), lambda b: (layer, b, 0, 0))
    rows = lambda c: c.reshape(depth, batch, past * n_heads, LANES)
    return pl.pallas_call(
        functools.partial(_sb_sample_body, n_heads=n_heads, tk=tk),
        grid=(batch,),
        in_specs=[new, new, new, old, old],
        out_specs=new,
        out_shape=jax.ShapeDtypeStruct((batch, seq, w), BF16),
        scratch_shapes=[pltpu.VMEM((seq, LANES), F32), pltpu.VMEM((seq, 1), F32)],
        compiler_params=_cparams(("parallel",)),
        name="sb_sample",
    )(q.reshape(batch, seq, w), k.reshape(batch, seq, w), v.reshape(batch, seq, w), rows(cache_k), rows(cache_v))


def _post_body(h_ref, gate_ref, ya_ref, yb_ref, yc_ref, p_ref,
               wpa_ref, wpb_ref, wpc_ref, wo_ref, nf_ref, wf1_ref, wf3_ref, wf2_ref,
               npl_ref, wpg_ref, wpp_ref, nfin_ref, o_ref, *, final, n_ffn_chunks):
    d = functools.partial(jnp.dot, preferred_element_type=F32)
    h = h_ref[...]
    dm = h.shape[1]
    gates = jax.nn.sigmoid(gate_ref[...])
    merged = (gates[:, :dm] * d(ya_ref[...], wpa_ref[...])
              + gates[:, dm:2 * dm] * d(yb_ref[...], wpb_ref[...])
              + gates[:, 2 * dm:] * d(yc_ref[...], wpc_ref[...]))
    h = h + d(merged.astype(BF16), wo_ref[...])
    xn = _rms(h, nf_ref[...]).astype(BF16)
    hid = wf1_ref.shape[1]
    cw = hid // n_ffn_chunks
    ff = None
    for c in range(n_ffn_chunks):
        cs = slice(c * cw, (c + 1) * cw)
        a1 = d(xn, wf1_ref[:, cs])
        a3 = d(xn, wf3_ref[:, cs])
        part = d((_silu(a1) * a3).astype(BF16), wf2_ref[cs, :])
        ff = part if ff is None else ff + part
    h = h + ff
    xn = _rms(h, npl_ref[...]).astype(BF16)
    h = h + jax.nn.sigmoid(d(xn, wpg_ref[...])) * d(p_ref[...].astype(BF16), wpp_ref[...])
    if final:
        h = _rms(h, nfin_ref[...])
    o_ref[...] = h


def _post(h, gate, ya, yb, yc, p_arr, layer, lw, norm_final, final, tm):
    ntok, dm = h.shape
    tm = min(tm, ntok)
    bw = ya.shape[1]
    pd = p_arr.shape[-1]
    p3 = p_arr.reshape(p_arr.shape[0], ntok, pd)
    tok = lambda w: pl.BlockSpec((tm, w), lambda i: (i, 0))
    weights = [lw['w_pa'], lw['w_pb'], lw['w_pc'], lw['w_o'], lw['norm_ffn'], lw['w_f1'], lw['w_f3'],
               lw['w_f2'], lw['norm_ple'], lw['w_pg'], lw['w_pp'], norm_final]
    hid = lw['w_f1'].shape[1]
    n_chunks = next(n for n in (2, 1) if hid % (n * LANES) == 0)
    return pl.pallas_call(
        functools.partial(_post_body, final=final, n_ffn_chunks=n_chunks),
        grid=(ntok // tm,),
        in_specs=[tok(dm), tok(gate.shape[1]), tok(bw), tok(bw), tok(bw),
                  pl.BlockSpec((None, tm, pd), lambda i: (layer, i, 0))]
                 + [_resident(w.shape) for w in weights],
        out_specs=tok(dm),
        out_shape=jax.ShapeDtypeStruct((ntok, dm), F32),
        compiler_params=_cparams(("parallel",)),
        name="post",
    )(h, gate, ya, yb, yc, p3, *weights)


def _layer_weights(i, a_log, dt_bias, w_in, conv_w, norm_mix, gdn_norm, w_pa, ln_v_g, ln_v_b, w_s, b_s,
                   w_pb, w_pc, w_o, norm_ffn, w_ffn_in, w_ffn_out, norm_ple, w_ple_gate, w_ple_proj,
                   dec_seq):
    n_heads = a_log.shape[1]
    c = conv_w.shape[2]
    vw = (c - 2 * n_heads * LANES)
    n_uv = 2 * w_s.shape[1] * LANES
    n_sb = w_pc.shape[1]
    o = 0
    w = w_in[i]
    w_a = w[:, o:o + c + vw]; o += c + vw
    w_ab = w[:, o:o + 2 * n_heads]; o += 2 * n_heads
    w_b = w[:, o:o + n_uv]; o += n_uv
    w_c = w[:, o:o + 3 * n_sb]; o += 3 * n_sb
    w_g = w[:, o:]
    row = lambda x: x.reshape(1, -1).astype(F32)
    lane_pad = lambda x: jnp.pad(x.reshape(1, -1), ((0, 0), (0, LANES - x.size))).astype(F32)
    hid = w_ffn_out.shape[1]
    span = w_s.shape[2]
    rep = ROWS // dec_seq
    lw = dict(
        n_heads=n_heads, n_qkv=c, n_z=vw, n_uv=n_uv, n_sb=n_sb, n_gate=w_g.shape[1],
        norm_mix=row(norm_mix[i]),
        w_a=w_a.astype(BF16), w_ab=jnp.pad(w_ab, ((0, 0), (0, LANES - 2 * n_heads))).astype(BF16),
        w_b=w_b.astype(BF16), w_c=w_c.astype(BF16), w_g=w_g.astype(BF16),
        conv_w=conv_w[i].astype(F32), a_log=lane_pad(a_log[i]), dt_bias=lane_pad(dt_bias[i]),
        gdn_norm=row(gdn_norm[i]),
        ln_v_g=row(ln_v_g[i]), ln_v_b=row(ln_v_b[i]),
        ws_p=w_s[i].astype(F32),
        bs_p=jnp.repeat(b_s[i].T, LANES, axis=1).astype(F32),
        ws_s=jnp.tile(w_s[i][:, :dec_seq, :dec_seq], (1, rep, rep)).astype(F32),
        bs_s=jnp.tile(jnp.repeat(b_s[i][:, :dec_seq].T, LANES, axis=1), (rep, 1)).astype(F32),
        w_pa=w_pa[i].astype(BF16), w_pb=w_pb[i].astype(BF16), w_pc=w_pc[i].astype(BF16),
        w_o=w_o[i].astype(BF16), norm_ffn=row(norm_ffn[i]),
        w_f1=w_ffn_in[i][:, :w_ffn_in.shape[2] // 2].astype(BF16),
        w_f3=w_ffn_in[i][:, w_ffn_in.shape[2] // 2:].astype(BF16),
        w_f2=w_ffn_out[i].astype(BF16), norm_ple=row(norm_ple[i]),
        w_pg=w_ple_gate[i].astype(BF16), w_pp=w_ple_proj[i].astype(BF16),
    )
    assert span == ROWS and hid == lw['w_f1'].shape[1]
    return lw


def _layer(h, p_arr, layer, conv_buf8, s0_arr, s0_index, caches, lw, norm_final, final, batch, seq, tm):
    n_heads = lw['n_heads']
    aqkv, az, ab, buv, qb, kb, vb, k4, v4, gate = _in_proj(h, lw['norm_mix'], lw, tm)
    ya, s_new = _gdn(aqkv, az, ab, conv_buf8, s0_arr, s0_index, lw, batch, seq)
    conv_new = aqkv.reshape(batch, seq, -1)[:, seq - 3:, :]
    if caches is None:
        (yb,) = _gmlp(buv, lw, ROWS, False)
        v_rows = None
        yc = _sb_prompt(qb, kb, vb, batch, seq, n_heads)
    else:
        yb, v_rows = _gmlp(buv, lw, seq, True)
        yc = _sb_sample(qb, kb, vb, caches[0], caches[1], layer, batch, seq, n_heads)
    bw = ya.shape[-1]
    h = _post(h, gate, ya.reshape(-1, bw), yb, yc.reshape(-1, bw), p_arr, layer, lw, norm_final, final, tm)
    kv_shape = (batch, seq, n_heads, LANES)
    return h, conv_new, s_new, k4.reshape(kv_shape), v4.reshape(kv_shape), v_rows


def kernel(x_prompt, x_sample, state_gdn_conv, state_gdn_s, cache_sb_k, cache_sb_v, p_prompt, p_sample, norm_mix, w_in, conv_w, a_log, dt_bias, gdn_norm, w_pa, ln_v_g, ln_v_b, w_s, b_s, w_pb, w_pc, w_o, norm_ffn, w_ffn_in, w_ffn_out, norm_ple, w_ple_gate, w_ple_proj, norm_final):
    depth = w_in.shape[0]
    bp, sp, dm = x_prompt.shape
    bs, ss, _ = x_sample.shape
    n_heads = a_log.shape[1]
    hp = x_prompt.reshape(bp * sp, dm)
    hs = x_sample.reshape(bs * ss, dm)
    nfin = norm_final.reshape(1, dm).astype(F32)
    c = conv_w.shape[2]
    zero_buf = jnp.zeros((bp, 8, c), F32)
    zero_s = jnp.zeros((bp, n_heads, LANES, LANES), F32)
    outs = [[] for _ in range(9)]
    for i in range(depth):
        lw = _layer_weights(i, a_log, dt_bias, w_in, conv_w, norm_mix, gdn_norm, w_pa, ln_v_g, ln_v_b, w_s,
                            b_s, w_pb, w_pc, w_o, norm_ffn, w_ffn_in, w_ffn_out, norm_ple, w_ple_gate,
                            w_ple_proj, ss)
        final = i == depth - 1
        hp, pc, ps, pk, pv, _ = _layer(hp, p_prompt, i, zero_buf, zero_s, lambda b, t: (b, 0, 0, 0), None,
                                       lw, nfin, final, bp, sp, 256)
        buf8 = jnp.pad(state_gdn_conv[i], ((0, 0), (8 - state_gdn_conv.shape[2], 0), (0, 0)))
        hs, sc, sn, sk, sv, sm = _layer(hs, p_sample, i, buf8, state_gdn_s,
                                        lambda b, t, i=i: (i, b, 0, 0, 0), (cache_sb_k, cache_sb_v),
                                        lw, nfin, final, bs, ss, 256)
        for lst, val in zip(outs, (pc, ps, pk, pv, sc, sn, sk, sv, sm.reshape(bs, ss, -1))):
            lst.append(val)
    stacked = [jnp.stack(l) for l in outs]
    return (hp.reshape(bp, sp, dm), hs.reshape(bs, ss, dm), *stacked)
```

```python
import functools
import math

import jax
import jax.numpy as jnp
from jax import lax
from jax.experimental import pallas as pl
from jax.experimental.pallas import tpu as pltpu

F32 = jnp.float32
BF16 = jnp.bfloat16

EPS = 1e-6
CHUNK = 64
LANES = 128
ROWS = 128
GDN_STREAMS = 2
SB_BLOCK = 256
SB_PROMPT_BLOCK = 256
SB_DEAD_LOG = -104.0
VMEM_LIMIT = 56 * 1024 * 1024


def _cparams(sem):
    return pltpu.CompilerParams(dimension_semantics=sem, vmem_limit_bytes=VMEM_LIMIT)


def _resident(shape):
    nd = len(shape)
    return pl.BlockSpec(shape, lambda *_: (0,) * nd, pipeline_mode=pl.Buffered(1))


def _rms(x, g):
    return x * lax.rsqrt(jnp.mean(x * x, axis=-1, keepdims=True) + EPS) * g


def _dot(a, b):
    return jnp.dot(a.astype(BF16), b.astype(BF16), preferred_element_type=F32)


def _dot_nt(a, b):
    return lax.dot_general(a.astype(BF16), b.astype(BF16), (((1,), (1,)), ((), ())),
                           preferred_element_type=F32)


def _dot_tn(a, b):
    return lax.dot_general(a.astype(BF16), b.astype(BF16), (((0,), (0,)), ((), ())),
                           preferred_element_type=F32)


def _split(x):
    hi = x.astype(BF16)
    lo = (x - hi.astype(F32)).astype(BF16)
    return hi, lo


def _blockdiag(x):
    z = jnp.zeros((x.shape[0], LANES), x.dtype)
    return jnp.concatenate([jnp.concatenate([x[:, :LANES], z], axis=1),
                            jnp.concatenate([z, x[:, LANES:]], axis=1)], axis=0)


def _pdot(a, b):
    return jnp.dot(a.astype(BF16), _blockdiag(b.astype(BF16)), preferred_element_type=F32)


def _pdot3(a, b):
    ah, al = _split(a)
    bh, bl = _split(b)
    bdh, bdl = _blockdiag(bh), _blockdiag(bl)
    d = functools.partial(jnp.dot, preferred_element_type=F32)
    return d(ah, bdh) + (d(al, bdh) + d(ah, bdl))


def _softplus(x):
    return jnp.maximum(x, 0.0) + jnp.log(1.0 + jnp.exp(-jnp.abs(x)))


def _silu(x):
    return x * jax.nn.sigmoid(x)


def _gmlp_tile(uv, ln_g, ln_b, ws_ref, bs, span):
    half = uv.shape[1] // 2
    ge = 0.5 * uv * (1.0 + lax.erf(uv * (2.0 ** -0.5)))
    u = ge[:, :half]
    v = ge[:, half:]
    mu = jnp.mean(v, axis=-1, keepdims=True)
    vc = v - mu
    vn = vc * lax.rsqrt(jnp.mean(vc * vc, axis=-1, keepdims=True) + EPS) * ln_g + ln_b
    row = lax.broadcasted_iota(jnp.int32, (ROWS, ROWS), 0)
    col = lax.broadcasted_iota(jnp.int32, (ROWS, ROWS), 1)
    mask = ((row & -span) == (col & -span)) & ((row & (span - 1) & -CHUNK) >= (col & (span - 1) & -CHUNK))
    ys = []
    for g in range(half // LANES):
        sl = slice(g * LANES, (g + 1) * LANES)
        w = jnp.where(mask, ws_ref[g], 0.0)
        ys.append(u[:, sl] * (_dot(w, vn[:, sl]) + bs[:, sl]))
    return jnp.concatenate(ys, axis=1), vn


def _in_proj_body(h_ref, g_ref, wa_ref, wab_ref, wb_ref, wc_ref, wg_ref, lng_ref, lnb_ref, ws_ref, bs_ref,
                  aqkv_ref, az_ref, ab_ref, yb_ref, qb_ref, kb_ref, vb_ref, k4_ref, v4_ref, gate_ref,
                  *vrows_ref, span):
    xn = _rms(h_ref[...], g_ref[...]).astype(BF16)
    d = functools.partial(jnp.dot, preferred_element_type=F32)
    tm = h_ref.shape[0]
    n_qkv = aqkv_ref.shape[1]
    buv = d(xn, wb_ref[...])
    for t in range(tm // ROWS):
        rs = slice(t * ROWS, (t + 1) * ROWS)
        y, vn = _gmlp_tile(buv[rs], lng_ref[...], lnb_ref[...], ws_ref, bs_ref[...], span)
        yb_ref[rs, :] = y.astype(yb_ref.dtype)
        if vrows_ref:
            vrows_ref[0][rs, :] = vn
    r = d(xn, wa_ref[...])
    aqkv_ref[...] = r[:, :n_qkv]
    az_ref[...] = r[:, n_qkv:]
    ab_ref[...] = d(xn, wab_ref[...])
    r = d(xn, wc_ref[...])
    w = qb_ref.shape[1]
    n_heads = w // LANES
    qb_ref[...] = r[:, :w].astype(BF16)
    kb_ref[...] = r[:, w:2 * w].astype(BF16)
    vb_ref[...] = r[:, 2 * w:].astype(BF16)
    for h in range(n_heads):
        k4_ref[pl.ds(h, tm, stride=n_heads), :] = r[:, w + h * LANES:w + (h + 1) * LANES]
        v4_ref[pl.ds(h, tm, stride=n_heads), :] = r[:, 2 * w + h * LANES:2 * w + (h + 1) * LANES]
    gate_ref[...] = d(xn, wg_ref[...])


def _in_proj(h, norm_g, lw, tm, span, want_rows):
    ntok, dm = h.shape
    tm = min(tm, ntok)
    assert tm % ROWS == 0
    n_heads = lw['n_sb'] // LANES
    half = lw['n_uv'] // 2
    tok = lambda w, dt=F32: (pl.BlockSpec((tm, w), lambda i: (i, 0)), jax.ShapeDtypeStruct((ntok, w), dt))
    kv4 = (pl.BlockSpec((tm * n_heads, LANES), lambda i: (i, 0)),
           jax.ShapeDtypeStruct((ntok * n_heads, LANES), F32))
    outs = [tok(lw['n_qkv']), tok(lw['n_z']), tok(LANES), tok(half, BF16),
            tok(lw['n_sb'], BF16), tok(lw['n_sb'], BF16), tok(lw['n_sb'], BF16), kv4, kv4, tok(lw['n_gate'])]
    if want_rows:
        outs.append(tok(half))
    ws, bs = (lw['ws_p'], lw['bs_p']) if span == ROWS else (lw['ws_s'], lw['bs_s'])
    return pl.pallas_call(
        functools.partial(_in_proj_body, span=span),
        grid=(ntok // tm,),
        in_specs=[tok(dm)[0], _resident((1, dm)), _resident(lw['w_a'].shape), _resident(lw['w_ab'].shape),
                  _resident(lw['w_b'].shape), _resident(lw['w_c'].shape), _resident(lw['w_g'].shape),
                  _resident((1, half)), _resident((1, half)), _resident(ws.shape), _resident(bs.shape)],
        out_specs=[o[0] for o in outs],
        out_shape=[o[1] for o in outs],
        compiler_params=_cparams(("parallel",)),
        name="in_proj",
    )(h, norm_g, lw['w_a'], lw['w_ab'], lw['w_b'], lw['w_c'], lw['w_g'], lw['ln_v_g'], lw['ln_v_b'], ws, bs)


def _gdn_stream(x, prev, ab, z_all, s_pairs, cw, alog, dtb, gn, *, t_real, n_heads, n_double):
    R = ROWS
    L2 = 2 * LANES
    qk_w = n_heads * LANES
    row = lax.broadcasted_iota(jnp.int32, (R, LANES), 0)
    valid = row < t_real
    row2 = lax.broadcasted_iota(jnp.int32, (R, L2), 0)
    col2 = lax.broadcasted_iota(jnp.int32, (R, L2), 1) & (LANES - 1)

    n_tap = cw.shape[0]
    row8 = lax.broadcasted_iota(jnp.int32, (8, x.shape[1]), 0)
    y = x * cw[n_tap - 1:n_tap, :]
    for sh in range(1, n_tap):
        xs = pltpu.roll(x, sh, 0)
        head = jnp.where(row8 < sh, pltpu.roll(prev, sh, 0), xs[:8])
        xs = jnp.concatenate([head, xs[8:]], axis=0)
        y = y + xs * cw[n_tap - 1 - sh:n_tap - sh, :]
    y = _silu(y)

    g = -jnp.exp(alog) * _softplus(ab + dtb)
    beta = jax.nn.sigmoid(ab)
    if t_real != R:
        g = jnp.where(valid, g, 0.0)
        beta = jnp.where(valid, beta, 0.0)
    rm = row & (CHUNK - 1)
    gc = g
    sh = 1
    while sh < CHUNK:
        gc = gc + jnp.where(rm >= sh, pltpu.roll(gc, sh, 0), 0.0)
        sh *= 2
    gc_t = gc.T
    e_gc = jnp.exp(gc)
    g_last = jnp.where(row < CHUNK, gc[CHUNK - 1:CHUNK, :], gc[R - 1:R, :])
    e_rest = jnp.exp(g_last - gc)

    same_blk = (row2 & -CHUNK) == (col2 & -CHUNK)
    incl = same_blk & (row2 >= col2)
    strict = same_blk & (row2 > col2)
    eye = (row2 == col2).astype(F32)
    zeros_half = jnp.zeros((CHUNK, L2), F32)

    def cols(m, h0):
        return jnp.concatenate([jnp.broadcast_to(m[:, h0:h0 + 1], (R, LANES)),
                                jnp.broadcast_to(m[:, h0 + 1:h0 + 2], (R, LANES))], axis=1)

    def l2n(m):
        return m * lax.rsqrt(jnp.sum(m * m, axis=-1, keepdims=True) + EPS)

    def chain(p):
        h0 = 2 * p
        ps = slice(h0 * LANES, (h0 + 2) * LANES)
        q0, q1 = y[:, h0 * LANES:(h0 + 1) * LANES], y[:, (h0 + 1) * LANES:(h0 + 2) * LANES]
        k0 = y[:, qk_w + h0 * LANES:qk_w + (h0 + 1) * LANES]
        k1 = y[:, qk_w + (h0 + 1) * LANES:qk_w + (h0 + 2) * LANES]
        qp = jnp.concatenate([l2n(q0), l2n(q1)], axis=1) * (LANES ** -0.5)
        kp = jnp.concatenate([l2n(k0), l2n(k1)], axis=1)
        vp = y[:, 2 * qk_w + h0 * LANES:2 * qk_w + (h0 + 2) * LANES]
        if t_real != R:
            ok = row2 < t_real
            qp, kp, vp = (jnp.where(ok, m, 0.0) for m in (qp, kp, vp))

        gcol = cols(gc, h0)
        grow = jnp.concatenate([gc_t[h0:h0 + 1, :], gc_t[h0 + 1:h0 + 2, :]], axis=1)
        bcol = cols(beta, n_heads + h0)
        ecol = cols(e_gc, h0)
        rcol = cols(e_rest, h0)

        decay = jnp.where(incl, jnp.exp(gcol - grow), 0.0)
        kq = _dot_nt(jnp.concatenate([kp, qp], axis=0), _blockdiag(kp))
        yield
        a_neg = jnp.where(strict, -(bcol * decay * kq[:R]), 0.0)
        p_mat = decay * kq[R:]

        t_inv = eye + a_neg
        pw = _pdot(a_neg, a_neg) if n_double else None
        yield
        for it in range(n_double):
            if it < n_double - 1:
                both = _pdot(jnp.concatenate([pw, t_inv], axis=0), pw)
                pw = both[:R]
                t_inv = t_inv + both[R:]
            else:
                t_inv = t_inv + _pdot(t_inv, pw)
            yield
        resid = (eye - t_inv) + _pdot3(a_neg, t_inv)
        yield
        t_inv = t_inv + _pdot(t_inv, resid)
        yield

        k_in = ecol * kp
        q_dec = ecol * qp
        k_dec = rcol * kp
        s_cur = s_pairs[p]
        outs = []
        for c in range(R // CHUNK):
            rs = slice(c * CHUNK, (c + 1) * CHUNK)
            ks_qs = _pdot(jnp.concatenate([k_in[rs], q_dec[rs]], axis=0), s_cur)
            yield
            rhs = bcol[rs] * (vp[rs] - ks_qs[:CHUNK])
            parts = [zeros_half] * (R // CHUNK)
            parts[c] = rhs
            u = _pdot3(t_inv[rs, :], jnp.concatenate(parts, axis=0))
            yield
            parts[c] = u
            u_full = jnp.concatenate(parts, axis=0)
            outs.append(ks_qs[CHUNK:] + _pdot(p_mat[rs, :], u_full))
            parts[c] = k_dec[rs]
            kd = jnp.concatenate(parts, axis=0)
            kd = jnp.concatenate([kd[:, :LANES], kd[:, LANES:]], axis=0)
            last = (c + 1) * CHUNK - 1
            bd = jnp.exp(jnp.concatenate([jnp.broadcast_to(gc[last:last + 1, h0:h0 + 1], (1, LANES)),
                                          jnp.broadcast_to(gc[last:last + 1, h0 + 1:h0 + 2], (1, LANES))],
                                         axis=1))
            s_cur = bd * s_cur + _dot_tn(kd, _blockdiag(u_full.astype(BF16)))
            yield
        o = jnp.concatenate(outs, axis=0)
        zg = _silu(z_all[:, ps])
        return jnp.concatenate([_rms(o[:, :LANES], gn), _rms(o[:, LANES:], gn)], axis=1) * zg, s_cur

    return [chain(p) for p in range(n_heads // 2)]


def _lockstep(gens):
    done = [None] * len(gens)
    live = list(range(len(gens)))
    while live:
        for i in list(live):
            try:
                next(gens[i])
            except StopIteration as stop:
                done[i] = stop.value
                live.remove(i)
    return done


def _gdn_body(qkv_ref, z_ref, ab_ref, cbuf_ref, s0_ref, cw_ref, alog_ref, dtb_ref, gn_ref,
              y_ref, snew_ref, xprev_sc, s_sc, *, t_real, n_heads, n_double):
    t = pl.program_id(1)
    nt = pl.num_programs(1)
    n_streams = qkv_ref.shape[0]
    n_pairs = n_heads // 2

    @pl.when(t == 0)
    def _():
        xprev_sc[...] = cbuf_ref[...]
        for s in range(n_streams):
            for p in range(n_pairs):
                s_sc[s, p] = jnp.concatenate([s0_ref[s, 2 * p], s0_ref[s, 2 * p + 1]], axis=1)

    def pad_rows(x):
        if t_real == ROWS:
            return x
        return jnp.concatenate([x, jnp.zeros((ROWS - t_real, x.shape[1]), x.dtype)], axis=0)

    chains = []
    for s in range(n_streams):
        x = pad_rows(qkv_ref[s])
        chains += _gdn_stream(x, xprev_sc[s], pad_rows(ab_ref[s]), pad_rows(z_ref[s]),
                              [s_sc[s, p] for p in range(n_pairs)], cw_ref[...], alog_ref[...],
                              dtb_ref[...], gn_ref[...], t_real=t_real, n_heads=n_heads, n_double=n_double)
        xprev_sc[s] = x[ROWS - 8:, :]
    done = _lockstep(chains)
    for s in range(n_streams):
        for p in range(n_pairs):
            y, s_new = done[s * n_pairs + p]
            s_sc[s, p] = s_new
            y_ref[s, :, 2 * p * LANES:2 * (p + 1) * LANES] = y[:t_real].astype(y_ref.dtype)

    @pl.when(t == nt - 1)
    def _():
        for s in range(n_streams):
            for p in range(n_pairs):
                sp = s_sc[s, p]
                snew_ref[s, 2 * p] = sp[:, :LANES]
                snew_ref[s, 2 * p + 1] = sp[:, LANES:]


def _gdn(aqkv, az, ab, conv_buf8, s0_arr, s0_index, lw, batch, seq):
    n_heads = lw['n_heads']
    ns = GDN_STREAMS
    assert batch % ns == 0 and n_heads % 2 == 0
    t_real = min(ROWS, seq)
    nt = seq // t_real
    c = aqkv.shape[1]
    vw = az.shape[1]
    n_double = max(int(math.ceil(math.log2(min(CHUNK, seq)))) - 1, 0)
    body = functools.partial(_gdn_body, t_real=t_real, n_heads=n_heads, n_double=n_double)
    tok = lambda w: pl.BlockSpec((ns, t_real, w), lambda b, t: (b, t, 0))
    s_shape = (ns, n_heads, LANES, LANES)
    return pl.pallas_call(
        body,
        grid=(batch // ns, nt),
        in_specs=[tok(c), tok(vw), tok(LANES),
                  pl.BlockSpec((ns, 8, c), lambda b, t: (b, 0, 0)),
                  pl.BlockSpec((None,) * (s0_arr.ndim - 4) + s_shape, s0_index),
                  _resident(lw['conv_w'].shape), _resident((1, LANES)), _resident((1, LANES)),
                  _resident((1, LANES))],
        out_specs=[tok(vw), pl.BlockSpec(s_shape, lambda b, t: (b, 0, 0, 0))],
        out_shape=[jax.ShapeDtypeStruct((batch, seq, vw), BF16),
                   jax.ShapeDtypeStruct((batch,) + s_shape[1:], F32)],
        scratch_shapes=[pltpu.VMEM((ns, 8, c), F32), pltpu.VMEM((ns, n_heads // 2, LANES, 2 * LANES), F32)],
        compiler_params=_cparams(("parallel", "arbitrary")),
        name="gdn",
    )(aqkv.reshape(batch, seq, c), az.reshape(batch, seq, vw), ab.reshape(batch, seq, LANES),
      conv_buf8, s0_arr, lw['conv_w'], lw['a_log'], lw['dt_bias'], lw['gdn_norm'])


def _sb_block_stages(q, kb, vb, carry, tri, mask):
    z = _dot_nt(q, kb) * (LANES ** -0.5)
    yield
    sp = _softplus(z)
    lf = -sp if mask is None else jnp.where(mask, -sp, 0.0)
    hi, lo = _split(lf)
    after = (jnp.dot(hi, tri, preferred_element_type=F32)
             + jnp.dot(lo, tri, preferred_element_type=F32))
    yield
    a = jnp.exp((z - sp) + after)
    if mask is not None:
        a = jnp.where(mask, a, 0.0)
    out = jnp.exp(carry) * _dot(a, vb)
    return out, jnp.sum(lf, axis=-1, keepdims=True)


def _sb_block(q, kb, vb, carry, tri, mask):
    return _lockstep([_sb_block_stages(q, kb, vb, carry, tri, mask)])[0]


def _sb_prompt_body(q_ref, k_ref, v_ref, o_ref, acc_sc, car_sc, *, n_heads):
    i = pl.program_id(1)
    tq = q_ref.shape[0]
    row = lax.broadcasted_iota(jnp.int32, (tq, tq), 0)
    col = lax.broadcasted_iota(jnp.int32, (tq, tq), 1)
    tri = (row > col).astype(BF16)
    heads = [slice(h * LANES, (h + 1) * LANES) for h in range(n_heads)]
    qs = [q_ref[:, sl] for sl in heads]

    def blocks(off, carries, mask):
        kb = k_ref[pl.ds(off, tq), :]
        vb = v_ref[pl.ds(off, tq), :]
        return _lockstep([_sb_block_stages(qs[h], kb[:, sl], vb[:, sl], carries[h], tri, mask)
                          for h, sl in enumerate(heads)])

    zero = jnp.zeros((tq, 1), F32)
    res = blocks(pl.multiple_of(i * tq, tq), [zero] * n_heads, col < row)
    live = None
    for h, (out, csum) in enumerate(res):
        acc_sc[h] = out
        car_sc[h] = csum
        m = jnp.max(csum)
        live = m if live is None else jnp.maximum(live, m)

    def cond(st):
        j, live = st
        return (j >= 0) & (live > SB_DEAD_LOG)

    def body(st):
        j, _ = st
        carries = [car_sc[h] for h in range(n_heads)]
        res = blocks(pl.multiple_of(j * tq, tq), carries, None)
        live = None
        for h, (out, csum) in enumerate(res):
            acc_sc[h] += out
            carry = carries[h] + csum
            car_sc[h] = carry
            m = jnp.max(carry)
            live = m if live is None else jnp.maximum(live, m)
        return j - 1, live

    lax.while_loop(cond, body, (i - 1, live))
    for h, sl in enumerate(heads):
        o_ref[:, sl] = acc_sc[h].astype(o_ref.dtype)


def _sb_prompt(q, k, v, batch, seq, n_heads):
    tq = min(SB_PROMPT_BLOCK, seq)
    w = n_heads * LANES
    blk = pl.BlockSpec((None, tq, w), lambda b, i: (b, i, 0))
    full = pl.BlockSpec((None, seq, w), lambda b, i: (b, 0, 0), pipeline_mode=pl.Buffered(1))
    return pl.pallas_call(
        functools.partial(_sb_prompt_body, n_heads=n_heads),
        grid=(batch, seq // tq),
        in_specs=[blk, full, full],
        out_specs=blk,
        out_shape=jax.ShapeDtypeStruct((batch, seq, w), BF16),
        scratch_shapes=[pltpu.VMEM((n_heads, tq, LANES), F32), pltpu.VMEM((n_heads, tq, 1), F32)],
        compiler_params=_cparams(("parallel", "arbitrary")),
        name="sb_prompt",
    )(q.reshape(batch, seq, w), k.reshape(batch, seq, w), v.reshape(batch, seq, w))


def _sb_sample_body(q_ref, k_ref, v_ref, kp_ref, vp_ref, o_ref, acc_sc, car_sc, *, n_heads, tk):
    tq = q_ref.shape[0]
    past = kp_ref.shape[0] // n_heads
    row = lax.broadcasted_iota(jnp.int32, (tq, LANES), 0)
    col = lax.broadcasted_iota(jnp.int32, (tq, LANES), 1)
    r2 = lax.broadcasted_iota(jnp.int32, (LANES, LANES), 0)
    c2 = lax.broadcasted_iota(jnp.int32, (LANES, LANES), 1)
    tri_new = (r2 > c2).astype(BF16)
    r3 = lax.broadcasted_iota(jnp.int32, (tk, tk), 0)
    c3 = lax.broadcasted_iota(jnp.int32, (tk, tk), 1)
    tri_past = (r3 > c3).astype(BF16)
    pad = jnp.zeros((LANES - tq, LANES), BF16)
    for h in range(n_heads):
        sl = slice(h * LANES, (h + 1) * LANES)
        q = q_ref[:, sl]
        kn = jnp.concatenate([k_ref[:, sl], pad], axis=0)
        vn = jnp.concatenate([v_ref[:, sl], pad], axis=0)
        out, csum = _sb_block(q, kn, vn, jnp.zeros((tq, 1), F32), tri_new, col < row)
        acc_sc[...] = out
        car_sc[...] = csum

        def cond(st):
            j, live = st
            return (j >= 0) & (live > SB_DEAD_LOG)

        def body(st, h=h, q=q):
            j, _ = st
            rows = pl.ds(j * (tk * n_heads) + h, tk, stride=n_heads)
            carry = car_sc[...]
            out, csum = _sb_block(q, kp_ref[rows, :], vp_ref[rows, :], carry, tri_past, None)
            acc_sc[...] += out
            carry = carry + csum
            car_sc[...] = carry
            return j - 1, jnp.max(carry)

        lax.while_loop(cond, body, (past // tk - 1, jnp.max(csum)))
        o_ref[:, sl] = acc_sc[...].astype(o_ref.dtype)


def _sb_sample(q, k, v, cache_k, cache_v, layer, batch, seq, n_heads):
    w = n_heads * LANES
    depth, _, past = cache_k.shape[:3]
    tk = min(SB_BLOCK, past)
    new = pl.BlockSpec((None, seq, w), lambda b: (b, 0, 0))
    old = pl.BlockSpec((None, None, past * n_heads, LANES), lambda b: (layer, b, 0, 0))
    rows = lambda c: c.reshape(depth, batch, past * n_heads, LANES)
    return pl.pallas_call(
        functools.partial(_sb_sample_body, n_heads=n_heads, tk=tk),
        grid=(batch,),
        in_specs=[new, new, new, old, old],
        out_specs=new,
        out_shape=jax.ShapeDtypeStruct((batch, seq, w), BF16),
        scratch_shapes=[pltpu.VMEM((seq, LANES), F32), pltpu.VMEM((seq, 1), F32)],
        compiler_params=_cparams(("parallel",)),
        name="sb_sample",
    )(q.reshape(batch, seq, w), k.reshape(batch, seq, w), v.reshape(batch, seq, w), rows(cache_k), rows(cache_v))


def _post_body(h_ref, gate_ref, ya_ref, yb_ref, yc_ref, p_ref,
               wpa_ref, wpb_ref, wpc_ref, wo_ref, nf_ref, wf1_ref, wf3_ref, wf2_ref,
               npl_ref, wpg_ref, wpp_ref, nfin_ref, o_ref, *, final, n_ffn_chunks):
    d = functools.partial(jnp.dot, preferred_element_type=F32)
    h = h_ref[...]
    dm = h.shape[1]
    gates = jax.nn.sigmoid(gate_ref[...])
    merged = (gates[:, :dm] * d(ya_ref[...], wpa_ref[...])
              + gates[:, dm:2 * dm] * d(yb_ref[...], wpb_ref[...])
              + gates[:, 2 * dm:] * d(yc_ref[...], wpc_ref[...]))
    h = h + d(merged.astype(BF16), wo_ref[...])
    xn = _rms(h, nf_ref[...]).astype(BF16)
    hid = wf1_ref.shape[1]
    cw = hid // n_ffn_chunks
    ff = None
    for c in range(n_ffn_chunks):
        cs = slice(c * cw, (c + 1) * cw)
        a1 = d(xn, wf1_ref[:, cs])
        a3 = d(xn, wf3_ref[:, cs])
        part = d((_silu(a1) * a3).astype(BF16), wf2_ref[cs, :])
        ff = part if ff is None else ff + part
    h = h + ff
    xn = _rms(h, npl_ref[...]).astype(BF16)
    h = h + jax.nn.sigmoid(d(xn, wpg_ref[...])) * d(p_ref[...].astype(BF16), wpp_ref[...])
    if final:
        h = _rms(h, nfin_ref[...])
    o_ref[...] = h


def _post(h, gate, ya, yb, yc, p_arr, layer, lw, norm_final, final, tm):
    ntok, dm = h.shape
    tm = min(tm, ntok)
    bw = ya.shape[1]
    pd = p_arr.shape[-1]
    p3 = p_arr.reshape(p_arr.shape[0], ntok, pd)
    tok = lambda w: pl.BlockSpec((tm, w), lambda i: (i, 0))
    weights = [lw['w_pa'], lw['w_pb'], lw['w_pc'], lw['w_o'], lw['norm_ffn'], lw['w_f1'], lw['w_f3'],
               lw['w_f2'], lw['norm_ple'], lw['w_pg'], lw['w_pp'], norm_final]
    hid = lw['w_f1'].shape[1]
    n_chunks = next(n for n in (2, 1) if hid % (n * LANES) == 0)
    return pl.pallas_call(
        functools.partial(_post_body, final=final, n_ffn_chunks=n_chunks),
        grid=(ntok // tm,),
        in_specs=[tok(dm), tok(gate.shape[1]), tok(bw), tok(bw), tok(bw),
                  pl.BlockSpec((None, tm, pd), lambda i: (layer, i, 0))]
                 + [_resident(w.shape) for w in weights],
        out_specs=tok(dm),
        out_shape=jax.ShapeDtypeStruct((ntok, dm), F32),
        compiler_params=_cparams(("parallel",)),
        name="post",
    )(h, gate, ya, yb, yc, p3, *weights)


def _layer_weights(i, a_log, dt_bias, w_in, conv_w, norm_mix, gdn_norm, w_pa, ln_v_g, ln_v_b, w_s, b_s,
                   w_pb, w_pc, w_o, norm_ffn, w_ffn_in, w_ffn_out, norm_ple, w_ple_gate, w_ple_proj,
                   dec_seq):
    n_heads = a_log.shape[1]
    c = conv_w.shape[2]
    vw = (c - 2 * n_heads * LANES)
    n_uv = 2 * w_s.shape[1] * LANES
    n_sb = w_pc.shape[1]
    o = 0
    w = w_in[i]
    w_a = w[:, o:o + c + vw]; o += c + vw
    w_ab = w[:, o:o + 2 * n_heads]; o += 2 * n_heads
    w_b = w[:, o:o + n_uv]; o += n_uv
    w_c = w[:, o:o + 3 * n_sb]; o += 3 * n_sb
    w_g = w[:, o:]
    row = lambda x: x.reshape(1, -1).astype(F32)
    lane_pad = lambda x: jnp.pad(x.reshape(1, -1), ((0, 0), (0, LANES - x.size))).astype(F32)
    hid = w_ffn_out.shape[1]
    span = w_s.shape[2]
    rep = ROWS // dec_seq
    lw = dict(
        n_heads=n_heads, n_qkv=c, n_z=vw, n_uv=n_uv, n_sb=n_sb, n_gate=w_g.shape[1],
        norm_mix=row(norm_mix[i]),
        w_a=w_a.astype(BF16), w_ab=jnp.pad(w_ab, ((0, 0), (0, LANES - 2 * n_heads))).astype(BF16),
        w_b=w_b.astype(BF16), w_c=w_c.astype(BF16), w_g=w_g.astype(BF16),
        conv_w=conv_w[i].astype(F32), a_log=lane_pad(a_log[i]), dt_bias=lane_pad(dt_bias[i]),
        gdn_norm=row(gdn_norm[i]),
        ln_v_g=row(ln_v_g[i]), ln_v_b=row(ln_v_b[i]),
        ws_p=w_s[i].astype(F32),
        bs_p=jnp.repeat(b_s[i].T, LANES, axis=1).astype(F32),
        ws_s=jnp.tile(w_s[i][:, :dec_seq, :dec_seq], (1, rep, rep)).astype(F32),
        bs_s=jnp.tile(jnp.repeat(b_s[i][:, :dec_seq].T, LANES, axis=1), (rep, 1)).astype(F32),
        w_pa=w_pa[i].astype(BF16), w_pb=w_pb[i].astype(BF16), w_pc=w_pc[i].astype(BF16),
        w_o=w_o[i].astype(BF16), norm_ffn=row(norm_ffn[i]),
        w_f1=w_ffn_in[i][:, :w_ffn_in.shape[2] // 2].astype(BF16),
        w_f3=w_ffn_in[i][:, w_ffn_in.shape[2] // 2:].astype(BF16),
        w_f2=w_ffn_out[i].astype(BF16), norm_ple=row(norm_ple[i]),
        w_pg=w_ple_gate[i].astype(BF16), w_pp=w_ple_proj[i].astype(BF16),
    )
    assert span == ROWS and hid == lw['w_f1'].shape[1]
    return lw


def _layer(h, p_arr, layer, conv_buf8, s0_arr, s0_index, caches, lw, norm_final, final, batch, seq, tm):
    n_heads = lw['n_heads']
    sample = caches is not None
    aqkv, az, ab, yb, qb, kb, vb, k4, v4, gate, *v_rows = _in_proj(
        h, lw['norm_mix'], lw, tm, min(seq, ROWS), sample)
    v_rows = v_rows[0] if sample else None
    ya, s_new = _gdn(aqkv, az, ab, conv_buf8, s0_arr, s0_index, lw, batch, seq)
    conv_new = aqkv.reshape(batch, seq, -1)[:, seq - 3:, :]
    if sample:
        yc = _sb_sample(qb, kb, vb, caches[0], caches[1], layer, batch, seq, n_heads)
    else:
        yc = _sb_prompt(qb, kb, vb, batch, seq, n_heads)
    bw = ya.shape[-1]
    h = _post(h, gate, ya.reshape(-1, bw), yb, yc.reshape(-1, bw), p_arr, layer, lw, norm_final, final, tm)
    kv_shape = (batch, seq, n_heads, LANES)
    return h, conv_new, s_new, k4.reshape(kv_shape), v4.reshape(kv_shape), v_rows


def kernel(x_prompt, x_sample, state_gdn_conv, state_gdn_s, cache_sb_k, cache_sb_v, p_prompt, p_sample, norm_mix, w_in, conv_w, a_log, dt_bias, gdn_norm, w_pa, ln_v_g, ln_v_b, w_s, b_s, w_pb, w_pc, w_o, norm_ffn, w_ffn_in, w_ffn_out, norm_ple, w_ple_gate, w_ple_proj, norm_final):
    depth = w_in.shape[0]
    bp, sp, dm = x_prompt.shape
    bs, ss, _ = x_sample.shape
    n_heads = a_log.shape[1]
    hp = x_prompt.reshape(bp * sp, dm)
    hs = x_sample.reshape(bs * ss, dm)
    nfin = norm_final.reshape(1, dm).astype(F32)
    c = conv_w.shape[2]
    zero_buf = jnp.zeros((bp, 8, c), F32)
    zero_s = jnp.zeros((bp, n_heads, LANES, LANES), F32)
    outs = [[] for _ in range(9)]
    for i in range(depth):
        lw = _layer_weights(i, a_log, dt_bias, w_in, conv_w, norm_mix, gdn_norm, w_pa, ln_v_g, ln_v_b, w_s,
                            b_s, w_pb, w_pc, w_o, norm_ffn, w_ffn_in, w_ffn_out, norm_ple, w_ple_gate,
                            w_ple_proj, ss)
        final = i == depth - 1
        hp, pc, ps, pk, pv, _ = _layer(hp, p_prompt, i, zero_buf, zero_s, lambda b, t: (b, 0, 0, 0), None,
                                       lw, nfin, final, bp, sp, 256)
        buf8 = jnp.pad(state_gdn_conv[i], ((0, 0), (8 - state_gdn_conv.shape[2], 0), (0, 0)))
        hs, sc, sn, sk, sv, sm = _layer(hs, p_sample, i, buf8, state_gdn_s,
                                        lambda b, t, i=i: (i, b, 0, 0, 0), (cache_sb_k, cache_sb_v),
                                        lw, nfin, final, bs, ss, 256)
        for lst, val in zip(outs, (pc, ps, pk, pv, sc, sn, sk, sv, sm.reshape(bs, ss, -1))):
            lst.append(val)
    stacked = [jnp.stack(l) for l in outs]
    return (hp.reshape(bp, sp, dm), hs.reshape(bs, ss, dm), *stacked)
```

```python
import functools
import math

import jax
import jax.numpy as jnp
from jax import lax
from jax.experimental import pallas as pl
from jax.experimental.pallas import tpu as pltpu

F32 = jnp.float32
BF16 = jnp.bfloat16

EPS = 1e-6
CHUNK = 64
LANES = 128
ROWS = 128
GDN_STREAMS = 2
SB_BLOCK = 256
SB_PROMPT_BLOCK = 256
SB_DEAD_LOG = -104.0
VMEM_LIMIT = 56 * 1024 * 1024


def _cparams(sem):
    return pltpu.CompilerParams(dimension_semantics=sem, vmem_limit_bytes=VMEM_LIMIT)


def _resident(shape):
    nd = len(shape)
    return pl.BlockSpec(shape, lambda *_: (0,) * nd, pipeline_mode=pl.Buffered(1))


def _rms(x, g):
    return x * lax.rsqrt(jnp.mean(x * x, axis=-1, keepdims=True) + EPS) * g


def _dot(a, b):
    return jnp.dot(a.astype(BF16), b.astype(BF16), preferred_element_type=F32)


def _dot_nt(a, b):
    return lax.dot_general(a.astype(BF16), b.astype(BF16), (((1,), (1,)), ((), ())),
                           preferred_element_type=F32)


def _dot_tn(a, b):
    return lax.dot_general(a.astype(BF16), b.astype(BF16), (((0,), (0,)), ((), ())),
                           preferred_element_type=F32)


def _split(x):
    hi = x.astype(BF16)
    lo = (x - hi.astype(F32)).astype(BF16)
    return hi, lo


def _blockdiag(x):
    z = jnp.zeros((x.shape[0], LANES), x.dtype)
    return jnp.concatenate([jnp.concatenate([x[:, :LANES], z], axis=1),
                            jnp.concatenate([z, x[:, LANES:]], axis=1)], axis=0)


def _pdot(a, b):
    return jnp.dot(a.astype(BF16), _blockdiag(b.astype(BF16)), preferred_element_type=F32)


def _pdot3(a, b):
    ah, al = _split(a)
    bh, bl = _split(b)
    bdh, bdl = _blockdiag(bh), _blockdiag(bl)
    d = functools.partial(jnp.dot, preferred_element_type=F32)
    return d(ah, bdh) + (d(al, bdh) + d(ah, bdl))


def _softplus(x):
    return jnp.maximum(x, 0.0) + jnp.log(1.0 + jnp.exp(-jnp.abs(x)))


def _silu(x):
    return x * jax.nn.sigmoid(x)


def _gmlp_tile(uv, ln_g, ln_b, ws_ref, bs, span):
    half = uv.shape[1] // 2
    ge = 0.5 * uv * (1.0 + lax.erf(uv * (2.0 ** -0.5)))
    u = ge[:, :half]
    v = ge[:, half:]
    mu = jnp.mean(v, axis=-1, keepdims=True)
    vc = v - mu
    vn = vc * lax.rsqrt(jnp.mean(vc * vc, axis=-1, keepdims=True) + EPS) * ln_g + ln_b
    row = lax.broadcasted_iota(jnp.int32, (ROWS, ROWS), 0)
    col = lax.broadcasted_iota(jnp.int32, (ROWS, ROWS), 1)
    mask = ((row & -span) == (col & -span)) & ((row & (span - 1) & -CHUNK) >= (col & (span - 1) & -CHUNK))
    ys = []
    for g in range(half // LANES):
        sl = slice(g * LANES, (g + 1) * LANES)
        w = jnp.where(mask, ws_ref[g], 0.0)
        ys.append(u[:, sl] * (_dot(w, vn[:, sl]) + bs[:, sl]))
    return jnp.concatenate(ys, axis=1), vn


def _in_proj_body(h_ref, g_ref, wa_ref, wab_ref, wb_ref, wc_ref, wg_ref, lng_ref, lnb_ref, ws_ref, bs_ref,
                  cbuf_ref, cw_ref,
                  aqkv_ref, az_ref, ab_ref, yb_ref, qb_ref, kb_ref, vb_ref, k4_ref, v4_ref, gate_ref,
                  extra_ref, xprev_sc, *, span, tiles_per_seq):
    xn = _rms(h_ref[...], g_ref[...]).astype(BF16)
    d = functools.partial(jnp.dot, preferred_element_type=F32)
    tm = h_ref.shape[0]
    n_qkv = aqkv_ref.shape[1]
    i = pl.program_id(0)
    if tiles_per_seq:
        @pl.when(i % tiles_per_seq == 0)
        def _():
            xprev_sc[...] = cbuf_ref[...]

    def gdn_stream():
        gdn_heads = (n_qkv - az_ref.shape[1]) // (2 * LANES)
        for kind, sl in _qkv_groups(gdn_heads, n_qkv):
            x = d(xn, wa_ref[:, sl])
            if tiles_per_seq:
                aqkv_ref[:, sl] = _conv_act(x, xprev_sc[:, sl], cw_ref[:, sl], kind)
                xprev_sc[:, sl] = x[tm - 8:, :]
                extra_ref[:, sl] = x[tm - 8:, :]
            else:
                aqkv_ref[:, sl] = x
            yield
        az_ref[...] = d(xn, wa_ref[:, n_qkv:])
        ab_ref[...] = d(xn, wab_ref[...])

    def gmlp_stream():
        buv = d(xn, wb_ref[...])
        yield
        for t in range(tm // ROWS):
            rs = slice(t * ROWS, (t + 1) * ROWS)
            y, vn = _gmlp_tile(buv[rs], lng_ref[...], lnb_ref[...], ws_ref, bs_ref[...], span)
            yb_ref[rs, :] = y.astype(yb_ref.dtype)
            if not tiles_per_seq:
                extra_ref[rs, :] = vn
            yield

    def sb_gate_stream():
        w = qb_ref.shape[1]
        n_heads = w // LANES
        for j, (b_ref, f_ref) in enumerate(((qb_ref, None), (kb_ref, k4_ref), (vb_ref, v4_ref))):
            r = d(xn, wc_ref[:, j * w:(j + 1) * w])
            b_ref[...] = r.astype(BF16)
            if f_ref is not None:
                for h in range(n_heads):
                    f_ref[pl.ds(h, tm, stride=n_heads), :] = r[:, h * LANES:(h + 1) * LANES]
            yield
        gw = gate_ref.shape[1]
        step = 4 * LANES
        for c0 in range(0, gw, step):
            gate_ref[:, c0:c0 + step] = d(xn, wg_ref[:, c0:c0 + step])
            yield

    _lockstep([gdn_stream(), gmlp_stream(), sb_gate_stream()])


def _in_proj(h, norm_g, lw, tm, batch, seq, conv_buf8):
    ntok, dm = h.shape
    tm = min(tm, ntok)
    assert tm % ROWS == 0
    tiles_per_seq = seq // tm if seq % tm == 0 else 0
    span = min(seq, ROWS)
    n_heads = lw['n_sb'] // LANES
    half = lw['n_uv'] // 2
    c = lw['n_qkv']
    tok = lambda w, dt=F32: (pl.BlockSpec((tm, w), lambda i: (i, 0)), jax.ShapeDtypeStruct((ntok, w), dt))
    kv4 = (pl.BlockSpec((tm * n_heads, LANES), lambda i: (i, 0)),
           jax.ShapeDtypeStruct((ntok * n_heads, LANES), F32))
    seq_of = (lambda i: (i // tiles_per_seq, 0, 0)) if tiles_per_seq else (lambda i: (0, 0, 0))
    outs = [tok(c), tok(lw['n_z']), tok(LANES), tok(half, BF16),
            tok(lw['n_sb'], BF16), tok(lw['n_sb'], BF16), tok(lw['n_sb'], BF16), kv4, kv4, tok(lw['n_gate'])]
    if tiles_per_seq:
        outs.append((pl.BlockSpec((None, 8, c), seq_of), jax.ShapeDtypeStruct((batch, 8, c), F32)))
    else:
        outs.append(tok(half))
    ws, bs = (lw['ws_p'], lw['bs_p']) if span == ROWS else (lw['ws_s'], lw['bs_s'])
    return pl.pallas_call(
        functools.partial(_in_proj_body, span=span, tiles_per_seq=tiles_per_seq),
        grid=(ntok // tm,),
        in_specs=[tok(dm)[0], _resident((1, dm)), _resident(lw['w_a'].shape), _resident(lw['w_ab'].shape),
                  _resident(lw['w_b'].shape), _resident(lw['w_c'].shape), _resident(lw['w_g'].shape),
                  _resident((1, half)), _resident((1, half)), _resident(ws.shape), _resident(bs.shape),
                  pl.BlockSpec((None, 8, c), seq_of), _resident(lw['conv_w'].shape)],
        out_specs=[o[0] for o in outs],
        out_shape=[o[1] for o in outs],
        scratch_shapes=[pltpu.VMEM((8, c), F32)],
        compiler_params=_cparams(("arbitrary",)),
        name="in_proj",
    )(h, norm_g, lw['w_a'], lw['w_ab'], lw['w_b'], lw['w_c'], lw['w_g'], lw['ln_v_g'], lw['ln_v_b'], ws, bs,
      conv_buf8, lw['conv_w'])


def _conv_act(x, prev, cw, kind):
    n_tap = cw.shape[0]
    row8 = lax.broadcasted_iota(jnp.int32, (8, x.shape[1]), 0)
    y = x * cw[n_tap - 1:n_tap, :]
    for sh in range(1, n_tap):
        xs = pltpu.roll(x, sh, 0)
        head = jnp.where(row8 < sh, pltpu.roll(prev, sh, 0), xs[:8])
        xs = jnp.concatenate([head, xs[8:]], axis=0)
        y = y + xs * cw[n_tap - 1 - sh:n_tap - sh, :]
    y = _silu(y)
    if kind == 'v':
        return y
    parts = []
    for h in range(x.shape[1] // LANES):
        m = y[:, h * LANES:(h + 1) * LANES]
        m = m * lax.rsqrt(jnp.sum(m * m, axis=-1, keepdims=True) + EPS)
        parts.append(m * (LANES ** -0.5) if kind == 'q' else m)
    return jnp.concatenate(parts, axis=1)


def _qkv_groups(n_heads, c):
    w = n_heads * LANES
    return (('q', slice(0, w)), ('k', slice(w, 2 * w)), ('v', slice(2 * w, c)))


def _qkv_act(x, prev, cw, n_heads):
    return jnp.concatenate([_conv_act(x[:, sl], prev[:, sl], cw[:, sl], kind)
                            for kind, sl in _qkv_groups(n_heads, x.shape[1])], axis=1)


def _gdn_stream(x, prev, ab, z_all, s_pairs, cw, alog, dtb, gn, *, t_real, n_heads, n_double, pre_act):
    R = ROWS
    L2 = 2 * LANES
    qk_w = n_heads * LANES
    row = lax.broadcasted_iota(jnp.int32, (R, LANES), 0)
    valid = row < t_real
    row2 = lax.broadcasted_iota(jnp.int32, (R, L2), 0)
    col2 = lax.broadcasted_iota(jnp.int32, (R, L2), 1) & (LANES - 1)

    y = x if pre_act else _qkv_act(x, prev, cw, n_heads)

    g = -jnp.exp(alog) * _softplus(ab + dtb)
    beta = jax.nn.sigmoid(ab)
    if t_real != R:
        g = jnp.where(valid, g, 0.0)
        beta = jnp.where(valid, beta, 0.0)
    rm = row & (CHUNK - 1)
    gc = g
    sh = 1
    while sh < CHUNK:
        gc = gc + jnp.where(rm >= sh, pltpu.roll(gc, sh, 0), 0.0)
        sh *= 2
    gc_t = gc.T
    e_gc = jnp.exp(gc)
    g_last = jnp.where(row < CHUNK, gc[CHUNK - 1:CHUNK, :], gc[R - 1:R, :])
    e_rest = jnp.exp(g_last - gc)

    same_blk = (row2 & -CHUNK) == (col2 & -CHUNK)
    incl = same_blk & (row2 >= col2)
    strict = same_blk & (row2 > col2)
    eye = (row2 == col2).astype(F32)
    zeros_half = jnp.zeros((CHUNK, L2), F32)

    def cols(m, h0):
        return jnp.concatenate([jnp.broadcast_to(m[:, h0:h0 + 1], (R, LANES)),
                                jnp.broadcast_to(m[:, h0 + 1:h0 + 2], (R, LANES))], axis=1)

    def chain(p):
        h0 = 2 * p
        ps = slice(h0 * LANES, (h0 + 2) * LANES)
        qp = y[:, h0 * LANES:(h0 + 2) * LANES]
        kp = y[:, qk_w + h0 * LANES:qk_w + (h0 + 2) * LANES]
        vp = y[:, 2 * qk_w + h0 * LANES:2 * qk_w + (h0 + 2) * LANES]
        if t_real != R:
            ok = row2 < t_real
            qp, kp, vp = (jnp.where(ok, m, 0.0) for m in (qp, kp, vp))

        gcol = cols(gc, h0)
        grow = jnp.concatenate([gc_t[h0:h0 + 1, :], gc_t[h0 + 1:h0 + 2, :]], axis=1)
        bcol = cols(beta, n_heads + h0)
        ecol = cols(e_gc, h0)
        rcol = cols(e_rest, h0)

        decay = jnp.where(incl, jnp.exp(gcol - grow), 0.0)
        kq = _dot_nt(jnp.concatenate([kp, qp], axis=0), _blockdiag(kp))
        yield
        a_neg = jnp.where(strict, -(bcol * decay * kq[:R]), 0.0)
        p_mat = decay * kq[R:]

        t_inv = eye + a_neg
        pw = _pdot(a_neg, a_neg) if n_double else None
        yield
        for it in range(n_double):
            if it < n_double - 1:
                both = _pdot(jnp.concatenate([pw, t_inv], axis=0), pw)
                pw = both[:R]
                t_inv = t_inv + both[R:]
            else:
                t_inv = t_inv + _pdot(t_inv, pw)
            yield
        resid = (eye - t_inv) + _pdot3(a_neg, t_inv)
        yield
        t_inv = t_inv + _pdot(t_inv, resid)
        yield

        k_in = ecol * kp
        q_dec = ecol * qp
        k_dec = rcol * kp
        s_cur = s_pairs[p]
        outs = []
        for c in range(R // CHUNK):
            rs = slice(c * CHUNK, (c + 1) * CHUNK)
            ks_qs = _pdot(jnp.concatenate([k_in[rs], q_dec[rs]], axis=0), s_cur)
            yield
            rhs = bcol[rs] * (vp[rs] - ks_qs[:CHUNK])
            parts = [zeros_half] * (R // CHUNK)
            parts[c] = rhs
            u = _pdot3(t_inv[rs, :], jnp.concatenate(parts, axis=0))
            yield
            parts[c] = u
            u_full = jnp.concatenate(parts, axis=0)
            outs.append(ks_qs[CHUNK:] + _pdot(p_mat[rs, :], u_full))
            parts[c] = k_dec[rs]
            kd = jnp.concatenate(parts, axis=0)
            kd = jnp.concatenate([kd[:, :LANES], kd[:, LANES:]], axis=0)
            last = (c + 1) * CHUNK - 1
            bd = jnp.exp(jnp.concatenate([jnp.broadcast_to(gc[last:last + 1, h0:h0 + 1], (1, LANES)),
                                          jnp.broadcast_to(gc[last:last + 1, h0 + 1:h0 + 2], (1, LANES))],
                                         axis=1))
            s_cur = bd * s_cur + _dot_tn(kd, _blockdiag(u_full.astype(BF16)))
            yield
        o = jnp.concatenate(outs, axis=0)
        zg = _silu(z_all[:, ps])
        return jnp.concatenate([_rms(o[:, :LANES], gn), _rms(o[:, LANES:], gn)], axis=1) * zg, s_cur

    return [chain(p) for p in range(n_heads // 2)]


def _lockstep(gens):
    done = [None] * len(gens)
    live = list(range(len(gens)))
    while live:
        for i in list(live):
            try:
                next(gens[i])
            except StopIteration as stop:
                done[i] = stop.value
                live.remove(i)
    return done


def _gdn_body(qkv_ref, z_ref, ab_ref, cbuf_ref, s0_ref, cw_ref, alog_ref, dtb_ref, gn_ref,
              y_ref, snew_ref, xprev_sc, s_sc, *, t_real, n_heads, n_double, pre_act):
    t = pl.program_id(1)
    nt = pl.num_programs(1)
    n_streams = qkv_ref.shape[0]
    n_pairs = n_heads // 2

    @pl.when(t == 0)
    def _():
        xprev_sc[...] = cbuf_ref[...]
        for s in range(n_streams):
            for p in range(n_pairs):
                s_sc[s, p] = jnp.concatenate([s0_ref[s, 2 * p], s0_ref[s, 2 * p + 1]], axis=1)

    def pad_rows(x):
        if t_real == ROWS:
            return x
        return jnp.concatenate([x, jnp.zeros((ROWS - t_real, x.shape[1]), x.dtype)], axis=0)

    chains = []
    for s in range(n_streams):
        x = pad_rows(qkv_ref[s])
        chains += _gdn_stream(x, xprev_sc[s], pad_rows(ab_ref[s]), pad_rows(z_ref[s]),
                              [s_sc[s, p] for p in range(n_pairs)], cw_ref[...], alog_ref[...],
                              dtb_ref[...], gn_ref[...], t_real=t_real, n_heads=n_heads, n_double=n_double,
                              pre_act=pre_act)
        if not pre_act:
            xprev_sc[s] = x[ROWS - 8:, :]
    done = _lockstep(chains)
    for s in range(n_streams):
        for p in range(n_pairs):
            y, s_new = done[s * n_pairs + p]
            s_sc[s, p] = s_new
            y_ref[s, :, 2 * p * LANES:2 * (p + 1) * LANES] = y[:t_real].astype(y_ref.dtype)

    @pl.when(t == nt - 1)
    def _():
        for s in range(n_streams):
            for p in range(n_pairs):
                sp = s_sc[s, p]
                snew_ref[s, 2 * p] = sp[:, :LANES]
                snew_ref[s, 2 * p + 1] = sp[:, LANES:]


def _gdn(aqkv, az, ab, conv_buf8, s0_arr, s0_index, lw, batch, seq, pre_act):
    n_heads = lw['n_heads']
    ns = GDN_STREAMS
    assert batch % ns == 0 and n_heads % 2 == 0
    t_real = min(ROWS, seq)
    nt = seq // t_real
    c = aqkv.shape[1]
    vw = az.shape[1]
    n_double = max(int(math.ceil(math.log2(min(CHUNK, seq)))) - 1, 0)
    body = functools.partial(_gdn_body, t_real=t_real, n_heads=n_heads, n_double=n_double, pre_act=pre_act)
    tok = lambda w: pl.BlockSpec((ns, t_real, w), lambda b, t: (b, t, 0))
    s_shape = (ns, n_heads, LANES, LANES)
    return pl.pallas_call(
        body,
        grid=(batch // ns, nt),
        in_specs=[tok(c), tok(vw), tok(LANES),
                  pl.BlockSpec((ns, 8, c), lambda b, t: (b, 0, 0)),
                  pl.BlockSpec((None,) * (s0_arr.ndim - 4) + s_shape, s0_index),
                  _resident(lw['conv_w'].shape), _resident((1, LANES)), _resident((1, LANES)),
                  _resident((1, LANES))],
        out_specs=[tok(vw), pl.BlockSpec(s_shape, lambda b, t: (b, 0, 0, 0))],
        out_shape=[jax.ShapeDtypeStruct((batch, seq, vw), BF16),
                   jax.ShapeDtypeStruct((batch,) + s_shape[1:], F32)],
        scratch_shapes=[pltpu.VMEM((ns, 8, c), F32), pltpu.VMEM((ns, n_heads // 2, LANES, 2 * LANES), F32)],
        compiler_params=_cparams(("parallel", "arbitrary")),
        name="gdn",
    )(aqkv.reshape(batch, seq, c), az.reshape(batch, seq, vw), ab.reshape(batch, seq, LANES),
      conv_buf8, s0_arr, lw['conv_w'], lw['a_log'], lw['dt_bias'], lw['gdn_norm'])


def _sb_block_stages(q, kb, vb, carry, tri, mask):
    z = _dot_nt(q, kb) * (LANES ** -0.5)
    yield
    sp = _softplus(z)
    lf = -sp if mask is None else jnp.where(mask, -sp, 0.0)
    hi, lo = _split(lf)
    after = (jnp.dot(hi, tri, preferred_element_type=F32)
             + jnp.dot(lo, tri, preferred_element_type=F32))
    yield
    a = jnp.exp((z - sp) + after)
    if mask is not None:
        a = jnp.where(mask, a, 0.0)
    out = jnp.exp(carry) * _dot(a, vb)
    return out, jnp.sum(lf, axis=-1, keepdims=True)


def _sb_block(q, kb, vb, carry, tri, mask):
    return _lockstep([_sb_block_stages(q, kb, vb, carry, tri, mask)])[0]


def _sb_prompt_body(q_ref, k_ref, v_ref, o_ref, acc_sc, car_sc, *, n_heads):
    i = pl.program_id(1)
    tq = q_ref.shape[0]
    row = lax.broadcasted_iota(jnp.int32, (tq, tq), 0)
    col = lax.broadcasted_iota(jnp.int32, (tq, tq), 1)
    tri = (row > col).astype(BF16)
    heads = [slice(h * LANES, (h + 1) * LANES) for h in range(n_heads)]
    qs = [q_ref[:, sl] for sl in heads]

    def blocks(off, carries, mask):
        kb = k_ref[pl.ds(off, tq), :]
        vb = v_ref[pl.ds(off, tq), :]
        return _lockstep([_sb_block_stages(qs[h], kb[:, sl], vb[:, sl], carries[h], tri, mask)
                          for h, sl in enumerate(heads)])

    zero = jnp.zeros((tq, 1), F32)
    res = blocks(pl.multiple_of(i * tq, tq), [zero] * n_heads, col < row)
    live = None
    for h, (out, csum) in enumerate(res):
        acc_sc[h] = out
        car_sc[h] = csum
        m = jnp.max(csum)
        live = m if live is None else jnp.maximum(live, m)

    def cond(st):
        j, live = st
        return (j >= 0) & (live > SB_DEAD_LOG)

    def body(st):
        j, _ = st
        carries = [car_sc[h] for h in range(n_heads)]
        res = blocks(pl.multiple_of(j * tq, tq), carries, None)
        live = None
        for h, (out, csum) in enumerate(res):
            acc_sc[h] += out
            carry = carries[h] + csum
            car_sc[h] = carry
            m = jnp.max(carry)
            live = m if live is None else jnp.maximum(live, m)
        return j - 1, live

    lax.while_loop(cond, body, (i - 1, live))
    for h, sl in enumerate(heads):
        o_ref[:, sl] = acc_sc[h].astype(o_ref.dtype)


def _sb_prompt(q, k, v, batch, seq, n_heads):
    tq = min(SB_PROMPT_BLOCK, seq)
    w = n_heads * LANES
    blk = pl.BlockSpec((None, tq, w), lambda b, i: (b, i, 0))
    full = pl.BlockSpec((None, seq, w), lambda b, i: (b, 0, 0), pipeline_mode=pl.Buffered(1))
    return pl.pallas_call(
        functools.partial(_sb_prompt_body, n_heads=n_heads),
        grid=(batch, seq // tq),
        in_specs=[blk, full, full],
        out_specs=blk,
        out_shape=jax.ShapeDtypeStruct((batch, seq, w), BF16),
        scratch_shapes=[pltpu.VMEM((n_heads, tq, LANES), F32), pltpu.VMEM((n_heads, tq, 1), F32)],
        compiler_params=_cparams(("parallel", "arbitrary")),
        name="sb_prompt",
    )(q.reshape(batch, seq, w), k.reshape(batch, seq, w), v.reshape(batch, seq, w))


def _sb_sample_body(q_ref, k_ref, v_ref, kp_hbm, vp_hbm, o_ref, kbuf, vbuf, sem, acc_sc, car_sc,
                    *, n_heads, tk, layer):
    b = pl.program_id(0)
    tq = q_ref.shape[0]
    blk_rows = tk * n_heads
    n_blk = kp_hbm.shape[2] // blk_rows

    def copies(j, slot):
        src = pl.ds(j * blk_rows, blk_rows)
        return (pltpu.make_async_copy(kp_hbm.at[layer, b, src, :], kbuf.at[slot], sem.at[0, slot]),
                pltpu.make_async_copy(vp_hbm.at[layer, b, src, :], vbuf.at[slot], sem.at[1, slot]))

    def slot_of(j):
        return lax.rem(n_blk - 1 - j, 2)

    for c in copies(n_blk - 1, 0):
        c.start()

    row = lax.broadcasted_iota(jnp.int32, (tq, LANES), 0)
    col = lax.broadcasted_iota(jnp.int32, (tq, LANES), 1)
    r2 = lax.broadcasted_iota(jnp.int32, (LANES, LANES), 0)
    c2 = lax.broadcasted_iota(jnp.int32, (LANES, LANES), 1)
    tri_new = (r2 > c2).astype(BF16)
    r3 = lax.broadcasted_iota(jnp.int32, (tk, tk), 0)
    c3 = lax.broadcasted_iota(jnp.int32, (tk, tk), 1)
    tri_past = (r3 > c3).astype(BF16)
    pad = jnp.zeros((LANES - tq, LANES), BF16)
    heads = [slice(h * LANES, (h + 1) * LANES) for h in range(n_heads)]
    qs = [q_ref[:, sl] for sl in heads]

    def fold(res, carries, first):
        live = None
        for h, (out, csum) in enumerate(res):
            if first:
                acc_sc[h] = out
                carry = csum
            else:
                acc_sc[h] += out
                carry = carries[h] + csum
            car_sc[h] = carry
            m = jnp.max(carry)
            live = m if live is None else jnp.maximum(live, m)
        return live

    zero = jnp.zeros((tq, 1), F32)
    live = fold(_lockstep([_sb_block_stages(qs[h], jnp.concatenate([k_ref[:, sl], pad], axis=0),
                                            jnp.concatenate([v_ref[:, sl], pad], axis=0),
                                            zero, tri_new, col < row)
                           for h, sl in enumerate(heads)]), None, True)

    def cond(st):
        j, live = st
        return (j >= 0) & (live > SB_DEAD_LOG)

    def body(st):
        j, _ = st
        slot = slot_of(j)
        for c in copies(j, slot):
            c.wait()

        @pl.when(j > 0)
        def _():
            for c in copies(j - 1, 1 - slot):
                c.start()

        carries = [car_sc[h] for h in range(n_heads)]
        res = _lockstep([_sb_block_stages(qs[h], kbuf[slot, pl.ds(h, tk, stride=n_heads), :],
                                          vbuf[slot, pl.ds(h, tk, stride=n_heads), :],
                                          carries[h], tri_past, None) for h in range(n_heads)])
        return j - 1, fold(res, carries, False)

    j_end, _ = lax.while_loop(cond, body, (n_blk - 1, live))

    @pl.when(j_end >= 0)
    def _():
        for c in copies(j_end, slot_of(j_end)):
            c.wait()

    for h, sl in enumerate(heads):
        o_ref[:, sl] = acc_sc[h].astype(o_ref.dtype)


def _sb_sample(q, k, v, cache_k, cache_v, layer, batch, seq, n_heads):
    w = n_heads * LANES
    depth, _, past = cache_k.shape[:3]
    tk = min(SB_BLOCK, past)
    assert past % tk == 0
    new = pl.BlockSpec((None, seq, w), lambda b: (b, 0, 0))
    hbm = pl.BlockSpec(memory_space=pl.ANY)
    rows = lambda c: c.reshape(depth, batch, past * n_heads, LANES)
    return pl.pallas_call(
        functools.partial(_sb_sample_body, n_heads=n_heads, tk=tk, layer=layer),
        grid=(batch,),
        in_specs=[new, new, new, hbm, hbm],
        out_specs=new,
        out_shape=jax.ShapeDtypeStruct((batch, seq, w), BF16),
        scratch_shapes=[pltpu.VMEM((2, tk * n_heads, LANES), F32), pltpu.VMEM((2, tk * n_heads, LANES), F32),
                        pltpu.SemaphoreType.DMA((2, 2)),
                        pltpu.VMEM((n_heads, seq, LANES), F32), pltpu.VMEM((n_heads, seq, 1), F32)],
        compiler_params=_cparams(("arbitrary",)),
        name="sb_sample",
    )(q.reshape(batch, seq, w), k.reshape(batch, seq, w), v.reshape(batch, seq, w), rows(cache_k), rows(cache_v))


def _post_body(h_ref, gate_ref, ya_ref, yb_ref, yc_ref, p_ref,
               wpa_ref, wpb_ref, wpc_ref, wo_ref, nf_ref, wf1_ref, wf3_ref, wf2_ref,
               npl_ref, wpg_ref, wpp_ref, nfin_ref, o_ref, *, final, n_ffn_chunks):
    d = functools.partial(jnp.dot, preferred_element_type=F32)
    h = h_ref[...]
    dm = h.shape[1]
    gates = jax.nn.sigmoid(gate_ref[...])
    merged = (gates[:, :dm] * d(ya_ref[...], wpa_ref[...])
              + gates[:, dm:2 * dm] * d(yb_ref[...], wpb_ref[...])
              + gates[:, 2 * dm:] * d(yc_ref[...], wpc_ref[...]))
    h = h + d(merged.astype(BF16), wo_ref[...])
    xn = _rms(h, nf_ref[...]).astype(BF16)
    hid = wf1_ref.shape[1]
    cw = hid // n_ffn_chunks
    ff = None
    for c in range(n_ffn_chunks):
        cs = slice(c * cw, (c + 1) * cw)
        a1 = d(xn, wf1_ref[:, cs])
        a3 = d(xn, wf3_ref[:, cs])
        part = d((_silu(a1) * a3).astype(BF16), wf2_ref[cs, :])
        ff = part if ff is None else ff + part
    h = h + ff
    xn = _rms(h, npl_ref[...]).astype(BF16)
    h = h + jax.nn.sigmoid(d(xn, wpg_ref[...])) * d(p_ref[...].astype(BF16), wpp_ref[...])
    if final:
        h = _rms(h, nfin_ref[...])
    o_ref[...] = h


def _post(h, gate, ya, yb, yc, p_arr, layer, lw, norm_final, final, tm):
    ntok, dm = h.shape
    tm = min(tm, ntok)
    bw = ya.shape[1]
    pd = p_arr.shape[-1]
    p3 = p_arr.reshape(p_arr.shape[0], ntok, pd)
    tok = lambda w: pl.BlockSpec((tm, w), lambda i: (i, 0))
    weights = [lw['w_pa'], lw['w_pb'], lw['w_pc'], lw['w_o'], lw['norm_ffn'], lw['w_f1'], lw['w_f3'],
               lw['w_f2'], lw['norm_ple'], lw['w_pg'], lw['w_pp'], norm_final]
    hid = lw['w_f1'].shape[1]
    n_chunks = next(n for n in (2, 1) if hid % (n * LANES) == 0)
    return pl.pallas_call(
        functools.partial(_post_body, final=final, n_ffn_chunks=n_chunks),
        grid=(ntok // tm,),
        in_specs=[tok(dm), tok(gate.shape[1]), tok(bw), tok(bw), tok(bw),
                  pl.BlockSpec((None, tm, pd), lambda i: (layer, i, 0))]
                 + [_resident(w.shape) for w in weights],
        out_specs=tok(dm),
        out_shape=jax.ShapeDtypeStruct((ntok, dm), F32),
        compiler_params=_cparams(("parallel",)),
        name="post",
    )(h, gate, ya, yb, yc, p3, *weights)


def _layer_weights(i, a_log, dt_bias, w_in, conv_w, norm_mix, gdn_norm, w_pa, ln_v_g, ln_v_b, w_s, b_s,
                   w_pb, w_pc, w_o, norm_ffn, w_ffn_in, w_ffn_out, norm_ple, w_ple_gate, w_ple_proj,
                   dec_seq):
    n_heads = a_log.shape[1]
    c = conv_w.shape[2]
    vw = (c - 2 * n_heads * LANES)
    n_uv = 2 * w_s.shape[1] * LANES
    n_sb = w_pc.shape[1]
    o = 0
    w = w_in[i]
    w_a = w[:, o:o + c + vw]; o += c + vw
    w_ab = w[:, o:o + 2 * n_heads]; o += 2 * n_heads
    w_b = w[:, o:o + n_uv]; o += n_uv
    w_c = w[:, o:o + 3 * n_sb]; o += 3 * n_sb
    w_g = w[:, o:]
    row = lambda x: x.reshape(1, -1).astype(F32)
    lane_pad = lambda x: jnp.pad(x.reshape(1, -1), ((0, 0), (0, LANES - x.size))).astype(F32)
    hid = w_ffn_out.shape[1]
    span = w_s.shape[2]
    rep = ROWS // dec_seq
    lw = dict(
        n_heads=n_heads, n_qkv=c, n_z=vw, n_uv=n_uv, n_sb=n_sb, n_gate=w_g.shape[1],
        norm_mix=row(norm_mix[i]),
        w_a=w_a.astype(BF16), w_ab=jnp.pad(w_ab, ((0, 0), (0, LANES - 2 * n_heads))).astype(BF16),
        w_b=w_b.astype(BF16), w_c=w_c.astype(BF16), w_g=w_g.astype(BF16),
        conv_w=conv_w[i].astype(F32), a_log=lane_pad(a_log[i]), dt_bias=lane_pad(dt_bias[i]),
        gdn_norm=row(gdn_norm[i]),
        ln_v_g=row(ln_v_g[i]), ln_v_b=row(ln_v_b[i]),
        ws_p=w_s[i].astype(F32),
        bs_p=jnp.repeat(b_s[i].T, LANES, axis=1).astype(F32),
        ws_s=jnp.tile(w_s[i][:, :dec_seq, :dec_seq], (1, rep, rep)).astype(F32),
        bs_s=jnp.tile(jnp.repeat(b_s[i][:, :dec_seq].T, LANES, axis=1), (rep, 1)).astype(F32),
        w_pa=w_pa[i].astype(BF16), w_pb=w_pb[i].astype(BF16), w_pc=w_pc[i].astype(BF16),
        w_o=w_o[i].astype(BF16), norm_ffn=row(norm_ffn[i]),
        w_f1=w_ffn_in[i][:, :w_ffn_in.shape[2] // 2].astype(BF16),
        w_f3=w_ffn_in[i][:, w_ffn_in.shape[2] // 2:].astype(BF16),
        w_f2=w_ffn_out[i].astype(BF16), norm_ple=row(norm_ple[i]),
        w_pg=w_ple_gate[i].astype(BF16), w_pp=w_ple_proj[i].astype(BF16),
    )
    assert span == ROWS and hid == lw['w_f1'].shape[1]
    return lw


def _layer(h, p_arr, layer, conv_buf8, s0_arr, s0_index, caches, lw, norm_final, final, batch, seq, tm):
    n_heads = lw['n_heads']
    sample = caches is not None
    aqkv, az, ab, yb, qb, kb, vb, k4, v4, gate, extra = _in_proj(
        h, lw['norm_mix'], lw, tm, batch, seq, conv_buf8)
    pre_act = extra.ndim == 3
    v_rows = None if pre_act else extra
    ya, s_new = _gdn(aqkv, az, ab, conv_buf8, s0_arr, s0_index, lw, batch, seq, pre_act)
    conv_new = extra[:, 5:, :] if pre_act else aqkv.reshape(batch, seq, -1)[:, seq - 3:, :]
    if sample:
        yc = _sb_sample(qb, kb, vb, caches[0], caches[1], layer, batch, seq, n_heads)
    else:
        yc = _sb_prompt(qb, kb, vb, batch, seq, n_heads)
    bw = ya.shape[-1]
    h = _post(h, gate, ya.reshape(-1, bw), yb, yc.reshape(-1, bw), p_arr, layer, lw, norm_final, final, tm)
    kv_shape = (batch, seq, n_heads, LANES)
    return h, conv_new, s_new, k4.reshape(kv_shape), v4.reshape(kv_shape), v_rows


def kernel(x_prompt, x_sample, state_gdn_conv, state_gdn_s, cache_sb_k, cache_sb_v, p_prompt, p_sample, norm_mix, w_in, conv_w, a_log, dt_bias, gdn_norm, w_pa, ln_v_g, ln_v_b, w_s, b_s, w_pb, w_pc, w_o, norm_ffn, w_ffn_in, w_ffn_out, norm_ple, w_ple_gate, w_ple_proj, norm_final):
    depth = w_in.shape[0]
    bp, sp, dm = x_prompt.shape
    bs, ss, _ = x_sample.shape
    n_heads = a_log.shape[1]
    hp = x_prompt.reshape(bp * sp, dm)
    hs = x_sample.reshape(bs * ss, dm)
    nfin = norm_final.reshape(1, dm).astype(F32)
    c = conv_w.shape[2]
    zero_buf = jnp.zeros((bp, 8, c), F32)
    zero_s = jnp.zeros((bp, n_heads, LANES, LANES), F32)
    outs = [[] for _ in range(9)]
    for i in range(depth):
        lw = _layer_weights(i, a_log, dt_bias, w_in, conv_w, norm_mix, gdn_norm, w_pa, ln_v_g, ln_v_b, w_s,
                            b_s, w_pb, w_pc, w_o, norm_ffn, w_ffn_in, w_ffn_out, norm_ple, w_ple_gate,
                            w_ple_proj, ss)
        final = i == depth - 1
        hp, pc, ps, pk, pv, _ = _layer(hp, p_prompt, i, zero_buf, zero_s, lambda b, t: (b, 0, 0, 0), None,
                                       lw, nfin, final, bp, sp, 256)
        buf8 = jnp.pad(state_gdn_conv[i], ((0, 0), (8 - state_gdn_conv.shape[2], 0), (0, 0)))
        hs, sc, sn, sk, sv, sm = _layer(hs, p_sample, i, buf8, state_gdn_s,
                                        lambda b, t, i=i: (i, b, 0, 0, 0), (cache_sb_k, cache_sb_v),
                                        lw, nfin, final, bs, ss, 256)
        for lst, val in zip(outs, (pc, ps, pk, pv, sc, sn, sk, sv, sm.reshape(bs, ss, -1))):
            lst.append(val)
    stacked = [jnp.stack(l) for l in outs]
    return (hp.reshape(bp, sp, dm), hs.reshape(bs, ss, dm), *stacked)
```

```python
import functools
import math

import jax
import jax.numpy as jnp
from jax import lax
from jax.experimental import pallas as pl
from jax.experimental.pallas import tpu as pltpu

F32 = jnp.float32
BF16 = jnp.bfloat16

EPS = 1e-6
LOG2E = 1.4426950408889634
CHUNK = 64
LANES = 128
ROWS = 128
GDN_TILES = 2
GDN_STREAMS = 2
SB_BLOCK = 256
SB_PROMPT_BLOCK = 256
SB_DEAD_LOG = -104.0
VMEM_LIMIT = 56 * 1024 * 1024


def _cparams(sem):
    return pltpu.CompilerParams(dimension_semantics=sem, vmem_limit_bytes=VMEM_LIMIT)


def _resident(shape):
    nd = len(shape)
    return pl.BlockSpec(shape, lambda *_: (0,) * nd, pipeline_mode=pl.Buffered(1))


def _rms(x, g):
    return x * lax.rsqrt(jnp.mean(x * x, axis=-1, keepdims=True) + EPS) * g


def _dot(a, b):
    return jnp.dot(a.astype(BF16), b.astype(BF16), preferred_element_type=F32)


def _dot_nt(a, b):
    return lax.dot_general(a.astype(BF16), b.astype(BF16), (((1,), (1,)), ((), ())),
                           preferred_element_type=F32)


def _dot_tn(a, b):
    return lax.dot_general(a.astype(BF16), b.astype(BF16), (((0,), (0,)), ((), ())),
                           preferred_element_type=F32)


def _split(x):
    hi = x.astype(BF16)
    lo = (x - hi.astype(F32)).astype(BF16)
    return hi, lo


def _blockdiag(x):
    z = jnp.zeros((x.shape[0], LANES), x.dtype)
    return jnp.concatenate([jnp.concatenate([x[:, :LANES], z], axis=1),
                            jnp.concatenate([z, x[:, LANES:]], axis=1)], axis=0)


def _pdot(a, b):
    return jnp.dot(a.astype(BF16), _blockdiag(b.astype(BF16)), preferred_element_type=F32)


def _pdot3(a, b):
    ah, al = _split(a)
    bh, bl = _split(b)
    bdh, bdl = _blockdiag(bh), _blockdiag(bl)
    d = functools.partial(jnp.dot, preferred_element_type=F32)
    return d(ah, bdh) + (d(al, bdh) + d(ah, bdl))


def _softplus(x):
    return jnp.maximum(x, 0.0) + jnp.log(1.0 + jnp.exp2(jnp.abs(x) * -LOG2E))


def _silu(x):
    return x * jax.nn.sigmoid(x)


def _gmlp_tile(uv, ln_g, ln_b, ws_ref, bs, span):
    half = uv.shape[1] // 2
    ge = 0.5 * uv * (1.0 + lax.erf(uv * (2.0 ** -0.5)))
    u = ge[:, :half]
    v = ge[:, half:]
    mu = jnp.mean(v, axis=-1, keepdims=True)
    vc = v - mu
    vn = vc * lax.rsqrt(jnp.mean(vc * vc, axis=-1, keepdims=True) + EPS) * ln_g + ln_b
    row = lax.broadcasted_iota(jnp.int32, (ROWS, ROWS), 0)
    col = lax.broadcasted_iota(jnp.int32, (ROWS, ROWS), 1)
    mask = ((row & -span) == (col & -span)) & ((row & (span - 1) & -CHUNK) >= (col & (span - 1) & -CHUNK))
    ys = []
    for g in range(half // LANES):
        sl = slice(g * LANES, (g + 1) * LANES)
        w = jnp.where(mask, ws_ref[g], 0.0)
        ys.append(u[:, sl] * (_dot(w, vn[:, sl]) + bs[:, sl]))
    return jnp.concatenate(ys, axis=1), vn


def _in_proj_body(h_ref, g_ref, wa_ref, wab_ref, wb_ref, wc_ref, wg_ref, lng_ref, lnb_ref, ws_ref, bs_ref,
                  cbuf_ref, cw_ref,
                  aqkv_ref, az_ref, ab_ref, yb_ref, qb_ref, kb_ref, vb_ref, k4_ref, v4_ref, gate_ref,
                  extra_ref, xprev_sc, *, span, tiles_per_seq):
    xn = _rms(h_ref[...], g_ref[...]).astype(BF16)
    d = functools.partial(jnp.dot, preferred_element_type=F32)
    tm = h_ref.shape[0]
    n_qkv = aqkv_ref.shape[1]
    i = pl.program_id(0)
    if tiles_per_seq:
        @pl.when(i % tiles_per_seq == 0)
        def _():
            xprev_sc[...] = cbuf_ref[...]

    def gdn_stream():
        gdn_heads = (n_qkv - az_ref.shape[1]) // (2 * LANES)
        for kind, sl in _qkv_groups(gdn_heads, n_qkv):
            x = d(xn, wa_ref[:, sl])
            if tiles_per_seq:
                aqkv_ref[:, sl] = _conv_act(x, xprev_sc[:, sl], cw_ref[:, sl], kind)
                xprev_sc[:, sl] = x[tm - 8:, :]
                extra_ref[:, sl] = x[tm - 8:, :]
            else:
                aqkv_ref[:, sl] = x
            yield
        az_ref[...] = d(xn, wa_ref[:, n_qkv:])
        ab_ref[...] = d(xn, wab_ref[...])

    def gmlp_stream():
        buv = d(xn, wb_ref[...])
        yield
        for t in range(tm // ROWS):
            rs = slice(t * ROWS, (t + 1) * ROWS)
            y, vn = _gmlp_tile(buv[rs], lng_ref[...], lnb_ref[...], ws_ref, bs_ref[...], span)
            yb_ref[rs, :] = y.astype(yb_ref.dtype)
            if not tiles_per_seq:
                extra_ref[rs, :] = vn
            yield

    def sb_gate_stream():
        w = qb_ref.shape[1]
        n_heads = w // LANES
        for j, (b_ref, f_ref) in enumerate(((qb_ref, None), (kb_ref, k4_ref), (vb_ref, v4_ref))):
            r = d(xn, wc_ref[:, j * w:(j + 1) * w])
            b_ref[...] = r.astype(BF16)
            if f_ref is not None:
                for h in range(n_heads):
                    f_ref[pl.ds(h, tm, stride=n_heads), :] = r[:, h * LANES:(h + 1) * LANES]
            yield
        gw = gate_ref.shape[1]
        step = 4 * LANES
        for c0 in range(0, gw, step):
            gate_ref[:, c0:c0 + step] = d(xn, wg_ref[:, c0:c0 + step])
            yield

    _lockstep([gdn_stream(), gmlp_stream(), sb_gate_stream()])


def _in_proj(h, norm_g, lw, tm, batch, seq, conv_buf8):
    ntok, dm = h.shape
    tm = min(tm, ntok)
    assert tm % ROWS == 0
    tiles_per_seq = seq // tm if seq % tm == 0 else 0
    span = min(seq, ROWS)
    n_heads = lw['n_sb'] // LANES
    half = lw['n_uv'] // 2
    c = lw['n_qkv']
    tok = lambda w, dt=F32: (pl.BlockSpec((tm, w), lambda i: (i, 0)), jax.ShapeDtypeStruct((ntok, w), dt))
    kv4 = (pl.BlockSpec((tm * n_heads, LANES), lambda i: (i, 0)),
           jax.ShapeDtypeStruct((ntok * n_heads, LANES), F32))
    seq_of = (lambda i: (i // tiles_per_seq, 0, 0)) if tiles_per_seq else (lambda i: (0, 0, 0))
    outs = [tok(c), tok(lw['n_z']), tok(LANES), tok(half, BF16),
            tok(lw['n_sb'], BF16), tok(lw['n_sb'], BF16), tok(lw['n_sb'], BF16), kv4, kv4, tok(lw['n_gate'])]
    if tiles_per_seq:
        outs.append((pl.BlockSpec((None, 8, c), seq_of), jax.ShapeDtypeStruct((batch, 8, c), F32)))
    else:
        outs.append(tok(half))
    ws, bs = (lw['ws_p'], lw['bs_p']) if span == ROWS else (lw['ws_s'], lw['bs_s'])
    return pl.pallas_call(
        functools.partial(_in_proj_body, span=span, tiles_per_seq=tiles_per_seq),
        grid=(ntok // tm,),
        in_specs=[tok(dm)[0], _resident((1, dm)), _resident(lw['w_a'].shape), _resident(lw['w_ab'].shape),
                  _resident(lw['w_b'].shape), _resident(lw['w_c'].shape), _resident(lw['w_g'].shape),
                  _resident((1, half)), _resident((1, half)), _resident(ws.shape), _resident(bs.shape),
                  pl.BlockSpec((None, 8, c), seq_of), _resident(lw['conv_w'].shape)],
        out_specs=[o[0] for o in outs],
        out_shape=[o[1] for o in outs],
        scratch_shapes=[pltpu.VMEM((8, c), F32)],
        compiler_params=_cparams(("arbitrary",)),
        name="in_proj",
    )(h, norm_g, lw['w_a'], lw['w_ab'], lw['w_b'], lw['w_c'], lw['w_g'], lw['ln_v_g'], lw['ln_v_b'], ws, bs,
      conv_buf8, lw['conv_w'])


def _conv_act(x, prev, cw, kind):
    n_tap = cw.shape[0]
    row8 = lax.broadcasted_iota(jnp.int32, (8, x.shape[1]), 0)
    y = x * cw[n_tap - 1:n_tap, :]
    for sh in range(1, n_tap):
        xs = pltpu.roll(x, sh, 0)
        head = jnp.where(row8 < sh, pltpu.roll(prev, sh, 0), xs[:8])
        xs = jnp.concatenate([head, xs[8:]], axis=0)
        y = y + xs * cw[n_tap - 1 - sh:n_tap - sh, :]
    y = _silu(y)
    if kind == 'v':
        return y
    parts = []
    for h in range(x.shape[1] // LANES):
        m = y[:, h * LANES:(h + 1) * LANES]
        m = m * lax.rsqrt(jnp.sum(m * m, axis=-1, keepdims=True) + EPS)
        parts.append(m * (LANES ** -0.5) if kind == 'q' else m)
    return jnp.concatenate(parts, axis=1)


def _qkv_groups(n_heads, c):
    w = n_heads * LANES
    return (('q', slice(0, w)), ('k', slice(w, 2 * w)), ('v', slice(2 * w, c)))


def _qkv_act(x, prev, cw, n_heads):
    return jnp.concatenate([_conv_act(x[:, sl], prev[:, sl], cw[:, sl], kind)
                            for kind, sl in _qkv_groups(n_heads, x.shape[1])], axis=1)


def _gdn_stream(x, prev, ab, z_all, s_pairs, s_next, cw, alog, dtb, gn, *, t_real, n_heads, n_double,
                pre_act):
    R = ROWS
    L2 = 2 * LANES
    qk_w = n_heads * LANES
    row = lax.broadcasted_iota(jnp.int32, (R, LANES), 0)
    valid = row < t_real
    row2 = lax.broadcasted_iota(jnp.int32, (R, L2), 0)
    col2 = lax.broadcasted_iota(jnp.int32, (R, L2), 1) & (LANES - 1)

    y = x if pre_act else _qkv_act(x, prev, cw, n_heads)

    g = -jnp.exp(alog) * _softplus(ab + dtb)
    beta = jax.nn.sigmoid(ab)
    if t_real != R:
        g = jnp.where(valid, g, 0.0)
        beta = jnp.where(valid, beta, 0.0)
    rm = row & (CHUNK - 1)
    gc = g
    sh = 1
    while sh < CHUNK:
        gc = gc + jnp.where(rm >= sh, pltpu.roll(gc, sh, 0), 0.0)
        sh *= 2
    gc_t = gc.T
    e_gc = jnp.exp(gc)
    g_last = jnp.where(row < CHUNK, gc[CHUNK - 1:CHUNK, :], gc[R - 1:R, :])
    e_rest = jnp.exp(g_last - gc)

    same_blk = (row2 & -CHUNK) == (col2 & -CHUNK)
    incl = same_blk & (row2 >= col2)
    strict = same_blk & (row2 > col2)
    eye = (row2 == col2).astype(F32)
    zeros_half = jnp.zeros((CHUNK, L2), F32)

    def cols(m, h0):
        return jnp.concatenate([jnp.broadcast_to(m[:, h0:h0 + 1], (R, LANES)),
                                jnp.broadcast_to(m[:, h0 + 1:h0 + 2], (R, LANES))], axis=1)

    def chain(p):
        h0 = 2 * p
        ps = slice(h0 * LANES, (h0 + 2) * LANES)
        qp = y[:, h0 * LANES:(h0 + 2) * LANES]
        kp = y[:, qk_w + h0 * LANES:qk_w + (h0 + 2) * LANES]
        vp = y[:, 2 * qk_w + h0 * LANES:2 * qk_w + (h0 + 2) * LANES]
        if t_real != R:
            ok = row2 < t_real
            qp, kp, vp = (jnp.where(ok, m, 0.0) for m in (qp, kp, vp))

        gcol = cols(gc, h0)
        grow = jnp.concatenate([gc_t[h0:h0 + 1, :], gc_t[h0 + 1:h0 + 2, :]], axis=1)
        bcol = cols(beta, n_heads + h0)
        ecol = cols(e_gc, h0)
        rcol = cols(e_rest, h0)

        decay = jnp.where(incl, jnp.exp(gcol - grow), 0.0)
        kq = _dot_nt(jnp.concatenate([kp, qp], axis=0), _blockdiag(kp))
        yield
        a_neg = jnp.where(strict, -(bcol * decay * kq[:R]), 0.0)
        p_mat = decay * kq[R:]

        t_inv = eye + a_neg
        pw = _pdot(a_neg, a_neg) if n_double else None
        yield
        for it in range(n_double):
            if it < n_double - 1:
                both = _pdot(jnp.concatenate([pw, t_inv], axis=0), pw)
                pw = both[:R]
                t_inv = t_inv + both[R:]
            else:
                t_inv = t_inv + _pdot(t_inv, pw)
            yield
        resid = (eye - t_inv) + _pdot3(a_neg, t_inv)
        yield
        t_inv = t_inv + _pdot(t_inv, resid)
        yield

        k_in = ecol * kp
        q_dec = ecol * qp
        k_dec = rcol * kp
        while s_pairs[p][0] is None:
            yield
        s_cur = s_pairs[p][0]
        outs = []
        n_live = -(-t_real // CHUNK)
        for c in range(n_live):
            rs = slice(c * CHUNK, (c + 1) * CHUNK)
            ks_qs = _pdot(jnp.concatenate([k_in[rs], q_dec[rs]], axis=0), s_cur)
            yield
            rhs = bcol[rs] * (vp[rs] - ks_qs[:CHUNK])
            parts = [zeros_half] * (R // CHUNK)
            parts[c] = rhs
            u = _pdot3(t_inv[rs, :], jnp.concatenate(parts, axis=0))
            yield
            parts[c] = u
            u_full = jnp.concatenate(parts, axis=0)
            outs.append(ks_qs[CHUNK:] + _pdot(p_mat[rs, :], u_full))
            parts[c] = k_dec[rs]
            kd = jnp.concatenate(parts, axis=0)
            kd = jnp.concatenate([kd[:, :LANES], kd[:, LANES:]], axis=0)
            last = (c + 1) * CHUNK - 1
            bd = jnp.exp(jnp.concatenate([jnp.broadcast_to(gc[last:last + 1, h0:h0 + 1], (1, LANES)),
                                          jnp.broadcast_to(gc[last:last + 1, h0 + 1:h0 + 2], (1, LANES))],
                                         axis=1))
            s_cur = bd * s_cur + _dot_tn(kd, _blockdiag(u_full.astype(BF16)))
            yield
        o = jnp.concatenate(outs, axis=0)
        zg = _silu(z_all[:n_live * CHUNK, ps])
        s_next[p][0] = s_cur
        return jnp.concatenate([_rms(o[:, :LANES], gn), _rms(o[:, LANES:], gn)], axis=1) * zg

    return [chain(p) for p in range(n_heads // 2)]


def _lockstep(gens):
    done = [None] * len(gens)
    live = list(range(len(gens)))
    while live:
        for i in list(live):
            try:
                next(gens[i])
            except StopIteration as stop:
                done[i] = stop.value
                live.remove(i)
    return done


def _gdn_body(qkv_ref, z_ref, ab_ref, cbuf_ref, s0_ref, cw_ref, alog_ref, dtb_ref, gn_ref,
              y_ref, snew_ref, xprev_sc, s_sc, *, t_real, n_heads, n_double, pre_act):
    t = pl.program_id(1)
    nt = pl.num_programs(1)
    n_streams = qkv_ref.shape[0]
    n_pairs = n_heads // 2

    @pl.when(t == 0)
    def _():
        xprev_sc[...] = cbuf_ref[...]
        for s in range(n_streams):
            for p in range(n_pairs):
                s_sc[s, p] = jnp.concatenate([s0_ref[s, 2 * p], s0_ref[s, 2 * p + 1]], axis=1)

    def pad_rows(x):
        if t_real == ROWS:
            return x
        return jnp.concatenate([x, jnp.zeros((ROWS - t_real, x.shape[1]), x.dtype)], axis=0)

    n_tiles = qkv_ref.shape[1] // t_real
    chains, last = [], []
    for s in range(n_streams):
        state = [[s_sc[s, p]] for p in range(n_pairs)]
        for k in range(n_tiles):
            rs = slice(k * t_real, (k + 1) * t_real)
            x = pad_rows(qkv_ref[s, rs, :])
            nxt = [[None] for _ in range(n_pairs)]
            chains += _gdn_stream(x, xprev_sc[s], pad_rows(ab_ref[s, rs, :]), pad_rows(z_ref[s, rs, :]),
                                  state, nxt, cw_ref[...], alog_ref[...], dtb_ref[...], gn_ref[...],
                                  t_real=t_real, n_heads=n_heads, n_double=n_double, pre_act=pre_act)
            state = nxt
            if not pre_act:
                xprev_sc[s] = x[ROWS - 8:, :]
        last.append(state)
    done = iter(_lockstep(chains))
    for s in range(n_streams):
        for k in range(n_tiles):
            for p in range(n_pairs):
                y_ref[s, k * t_real:(k + 1) * t_real, 2 * p * LANES:2 * (p + 1) * LANES] = (
                    next(done)[:t_real].astype(y_ref.dtype))
        for p in range(n_pairs):
            s_sc[s, p] = last[s][p][0]

    @pl.when(t == nt - 1)
    def _():
        for s in range(n_streams):
            for p in range(n_pairs):
                sp = s_sc[s, p]
                snew_ref[s, 2 * p] = sp[:, :LANES]
                snew_ref[s, 2 * p + 1] = sp[:, LANES:]


def _gdn(aqkv, az, ab, conv_buf8, s0_arr, s0_index, lw, batch, seq, pre_act):
    n_heads = lw['n_heads']
    ns = GDN_STREAMS
    assert batch % ns == 0 and n_heads % 2 == 0
    t_real = min(ROWS, seq)
    step_rows = t_real * (GDN_TILES if seq % (ROWS * GDN_TILES) == 0 else 1)
    nt = seq // step_rows
    c = aqkv.shape[1]
    vw = az.shape[1]
    n_double = max(int(math.ceil(math.log2(min(CHUNK, seq)))) - 1, 0)
    body = functools.partial(_gdn_body, t_real=t_real, n_heads=n_heads, n_double=n_double, pre_act=pre_act)
    tok = lambda w: pl.BlockSpec((ns, step_rows, w), lambda b, t: (b, t, 0))
    s_shape = (ns, n_heads, LANES, LANES)
    return pl.pallas_call(
        body,
        grid=(batch // ns, nt),
        in_specs=[tok(c), tok(vw), tok(LANES),
                  pl.BlockSpec((ns, 8, c), lambda b, t: (b, 0, 0)),
                  pl.BlockSpec((None,) * (s0_arr.ndim - 4) + s_shape, s0_index),
                  _resident(lw['conv_w'].shape), _resident((1, LANES)), _resident((1, LANES)),
                  _resident((1, LANES))],
        out_specs=[tok(vw), pl.BlockSpec(s_shape, lambda b, t: (b, 0, 0, 0))],
        out_shape=[jax.ShapeDtypeStruct((batch, seq, vw), BF16),
                   jax.ShapeDtypeStruct((batch,) + s_shape[1:], F32)],
        scratch_shapes=[pltpu.VMEM((ns, 8, c), F32), pltpu.VMEM((ns, n_heads // 2, LANES, 2 * LANES), F32)],
        compiler_params=_cparams(("parallel", "arbitrary")),
        name="gdn",
    )(aqkv.reshape(batch, seq, c), az.reshape(batch, seq, vw), ab.reshape(batch, seq, LANES),
      conv_buf8, s0_arr, lw['conv_w'], lw['a_log'], lw['dt_bias'], lw['gdn_norm'])


def _sb_block_stages(q, kb, vb, carry, tri, mask):
    z = _dot_nt(q, kb) * (LANES ** -0.5)
    yield
    sp = _softplus(z)
    lf = -sp if mask is None else jnp.where(mask, -sp, 0.0)
    hi, lo = _split(lf)
    after = (jnp.dot(hi, tri, preferred_element_type=F32)
             + jnp.dot(lo, tri, preferred_element_type=F32))
    yield
    a = jnp.exp((z - sp) + after)
    if mask is not None:
        a = jnp.where(mask, a, 0.0)
    out = jnp.exp(carry) * _dot(a, vb)
    return out, jnp.sum(lf, axis=-1, keepdims=True)


def _sb_block(q, kb, vb, carry, tri, mask):
    return _lockstep([_sb_block_stages(q, kb, vb, carry, tri, mask)])[0]


def _sb_prompt_body(q_ref, k_ref, v_ref, o_ref, acc_sc, car_sc, *, n_heads):
    i = pl.program_id(1)
    tq = q_ref.shape[0]
    row = lax.broadcasted_iota(jnp.int32, (tq, tq), 0)
    col = lax.broadcasted_iota(jnp.int32, (tq, tq), 1)
    tri = (row > col).astype(BF16)
    heads = [slice(h * LANES, (h + 1) * LANES) for h in range(n_heads)]
    qs = [q_ref[:, sl] for sl in heads]

    def blocks(off, carries, mask):
        kb = k_ref[pl.ds(off, tq), :]
        vb = v_ref[pl.ds(off, tq), :]
        return _lockstep([_sb_block_stages(qs[h], kb[:, sl], vb[:, sl], carries[h], tri, mask)
                          for h, sl in enumerate(heads)])

    zero = jnp.zeros((tq, 1), F32)
    res = blocks(pl.multiple_of(i * tq, tq), [zero] * n_heads, col < row)
    live = None
    for h, (out, csum) in enumerate(res):
        acc_sc[h] = out
        car_sc[h] = csum
        m = jnp.max(csum)
        live = m if live is None else jnp.maximum(live, m)

    def cond(st):
        j, live = st
        return (j >= 0) & (live > SB_DEAD_LOG)

    def body(st):
        j, _ = st
        carries = [car_sc[h] for h in range(n_heads)]
        res = blocks(pl.multiple_of(j * tq, tq), carries, None)
        live = None
        for h, (out, csum) in enumerate(res):
            acc_sc[h] += out
            carry = carries[h] + csum
            car_sc[h] = carry
            m = jnp.max(carry)
            live = m if live is None else jnp.maximum(live, m)
        return j - 1, live

    lax.while_loop(cond, body, (i - 1, live))
    for h, sl in enumerate(heads):
        o_ref[:, sl] = acc_sc[h].astype(o_ref.dtype)


def _sb_prompt(q, k, v, batch, seq, n_heads):
    tq = min(SB_PROMPT_BLOCK, seq)
    w = n_heads * LANES
    blk = pl.BlockSpec((None, tq, w), lambda b, i: (b, i, 0))
    full = pl.BlockSpec((None, seq, w), lambda b, i: (b, 0, 0), pipeline_mode=pl.Buffered(1))
    return pl.pallas_call(
        functools.partial(_sb_prompt_body, n_heads=n_heads),
        grid=(batch, seq // tq),
        in_specs=[blk, full, full],
        out_specs=blk,
        out_shape=jax.ShapeDtypeStruct((batch, seq, w), BF16),
        scratch_shapes=[pltpu.VMEM((n_heads, tq, LANES), F32), pltpu.VMEM((n_heads, tq, 1), F32)],
        compiler_params=_cparams(("parallel", "arbitrary")),
        name="sb_prompt",
    )(q.reshape(batch, seq, w), k.reshape(batch, seq, w), v.reshape(batch, seq, w))


def _sb_sample_body(q_ref, k_ref, v_ref, kp_hbm, vp_hbm, o_ref, kbuf, vbuf, sem, acc_sc, car_sc,
                    *, n_heads, tk, layer):
    b = pl.program_id(0)
    n_streams = pl.num_programs(0)
    tq = q_ref.shape[0]
    blk_rows = tk * n_heads
    n_blk = kp_hbm.shape[2] // blk_rows
    newest = n_blk - 1

    def copies(stream, j, slot):
        src = pl.ds(j * blk_rows, blk_rows)
        return (pltpu.make_async_copy(kp_hbm.at[layer, stream, src, :], kbuf.at[slot], sem.at[0, slot]),
                pltpu.make_async_copy(vp_hbm.at[layer, stream, src, :], vbuf.at[slot], sem.at[1, slot]))

    def start(stream, j, slot):
        for c in copies(stream, j, slot):
            c.start()

    def wait(stream, j, slot):
        for c in copies(stream, j, slot):
            c.wait()

    def slot_of(j):
        return jnp.where(j == newest, 2 + lax.rem(b, 2), lax.rem(newest - 1 - j, 2))

    @pl.when(b == 0)
    def _():
        start(b, newest, 2)

    @pl.when(b + 1 < n_streams)
    def _():
        start(b + 1, newest, 2 + lax.rem(b + 1, 2))

    if n_blk > 1:
        start(b, newest - 1, 0)

    row = lax.broadcasted_iota(jnp.int32, (tq, LANES), 0)
    col = lax.broadcasted_iota(jnp.int32, (tq, LANES), 1)
    r2 = lax.broadcasted_iota(jnp.int32, (LANES, LANES), 0)
    c2 = lax.broadcasted_iota(jnp.int32, (LANES, LANES), 1)
    tri_new = (r2 > c2).astype(BF16)
    r3 = lax.broadcasted_iota(jnp.int32, (tk, tk), 0)
    c3 = lax.broadcasted_iota(jnp.int32, (tk, tk), 1)
    tri_past = (r3 > c3).astype(BF16)
    pad = jnp.zeros((LANES - tq, LANES), BF16)
    heads = [slice(h * LANES, (h + 1) * LANES) for h in range(n_heads)]
    qs = [q_ref[:, sl] for sl in heads]

    def fold(res, carries, first):
        live = None
        for h, (out, csum) in enumerate(res):
            if first:
                acc_sc[h] = out
                carry = csum
            else:
                acc_sc[h] += out
                carry = carries[h] + csum
            car_sc[h] = carry
            m = jnp.max(carry)
            live = m if live is None else jnp.maximum(live, m)
        return live

    zero = jnp.zeros((tq, 1), F32)
    live = fold(_lockstep([_sb_block_stages(qs[h], jnp.concatenate([k_ref[:, sl], pad], axis=0),
                                            jnp.concatenate([v_ref[:, sl], pad], axis=0),
                                            zero, tri_new, col < row)
                           for h, sl in enumerate(heads)]), None, True)

    def cond(st):
        j, live = st
        return (j >= 0) & (live > SB_DEAD_LOG)

    def body(st):
        j, _ = st
        slot = slot_of(j)
        wait(b, j, slot)

        @pl.when((j > 0) & (j < newest))
        def _():
            start(b, j - 1, slot_of(j - 1))

        carries = [car_sc[h] for h in range(n_heads)]
        res = _lockstep([_sb_block_stages(qs[h], kbuf[slot, pl.ds(h, tk, stride=n_heads), :],
                                          vbuf[slot, pl.ds(h, tk, stride=n_heads), :],
                                          carries[h], tri_past, None) for h in range(n_heads)])
        return j - 1, fold(res, carries, False)

    j_end, _ = lax.while_loop(cond, body, (newest, live))

    @pl.when(j_end == newest)
    def _():
        wait(b, newest, slot_of(newest))

    if n_blk > 1:
        @pl.when(j_end >= 0)
        def _():
            ahead = jnp.minimum(j_end, newest - 1)
            wait(b, ahead, slot_of(ahead))

    for h, sl in enumerate(heads):
        o_ref[:, sl] = acc_sc[h].astype(o_ref.dtype)


def _sb_sample(q, k, v, cache_k, cache_v, layer, batch, seq, n_heads):
    w = n_heads * LANES
    depth, _, past = cache_k.shape[:3]
    tk = min(SB_BLOCK, past)
    assert past % tk == 0
    new = pl.BlockSpec((None, seq, w), lambda b: (b, 0, 0))
    hbm = pl.BlockSpec(memory_space=pl.ANY)
    rows = lambda c: c.reshape(depth, batch, past * n_heads, LANES)
    return pl.pallas_call(
        functools.partial(_sb_sample_body, n_heads=n_heads, tk=tk, layer=layer),
        grid=(batch,),
        in_specs=[new, new, new, hbm, hbm],
        out_specs=new,
        out_shape=jax.ShapeDtypeStruct((batch, seq, w), BF16),
        scratch_shapes=[pltpu.VMEM((4, tk * n_heads, LANES), F32), pltpu.VMEM((4, tk * n_heads, LANES), F32),
                        pltpu.SemaphoreType.DMA((2, 4)),
                        pltpu.VMEM((n_heads, seq, LANES), F32), pltpu.VMEM((n_heads, seq, 1), F32)],
        compiler_params=_cparams(("arbitrary",)),
        name="sb_sample",
    )(q.reshape(batch, seq, w), k.reshape(batch, seq, w), v.reshape(batch, seq, w), rows(cache_k), rows(cache_v))


def _post_body(h_ref, gate_ref, ya_ref, yb_ref, yc_ref, p_ref,
               wpa_ref, wpb_ref, wpc_ref, wo_ref, nf_ref, wf1_ref, wf3_ref, wf2_ref,
               npl_ref, wpg_ref, wpp_ref, nfin_ref, o_ref, *, final, n_ffn_chunks):
    d = functools.partial(jnp.dot, preferred_element_type=F32)
    h = h_ref[...]
    dm = h.shape[1]
    gates = jax.nn.sigmoid(gate_ref[...])
    merged = (gates[:, :dm] * d(ya_ref[...], wpa_ref[...])
              + gates[:, dm:2 * dm] * d(yb_ref[...], wpb_ref[...])
              + gates[:, 2 * dm:] * d(yc_ref[...], wpc_ref[...]))
    h = h + d(merged.astype(BF16), wo_ref[...])
    xn = _rms(h, nf_ref[...]).astype(BF16)
    hid = wf1_ref.shape[1]
    cw = hid // n_ffn_chunks
    ff = None
    for c in range(n_ffn_chunks):
        cs = slice(c * cw, (c + 1) * cw)
        a1 = d(xn, wf1_ref[:, cs])
        a3 = d(xn, wf3_ref[:, cs])
        part = d((_silu(a1) * a3).astype(BF16), wf2_ref[cs, :])
        ff = part if ff is None else ff + part
    h = h + ff
    xn = _rms(h, npl_ref[...]).astype(BF16)
    h = h + jax.nn.sigmoid(d(xn, wpg_ref[...])) * d(p_ref[...].astype(BF16), wpp_ref[...])
    if final:
        h = _rms(h, nfin_ref[...])
    o_ref[...] = h


def _post(h, gate, ya, yb, yc, p_arr, layer, lw, norm_final, final, tm):
    ntok, dm = h.shape
    tm = min(tm, ntok)
    bw = ya.shape[1]
    pd = p_arr.shape[-1]
    p3 = p_arr.reshape(p_arr.shape[0], ntok, pd)
    tok = lambda w: pl.BlockSpec((tm, w), lambda i: (i, 0))
    weights = [lw['w_pa'], lw['w_pb'], lw['w_pc'], lw['w_o'], lw['norm_ffn'], lw['w_f1'], lw['w_f3'],
               lw['w_f2'], lw['norm_ple'], lw['w_pg'], lw['w_pp'], norm_final]
    hid = lw['w_f1'].shape[1]
    n_chunks = next(n for n in (2, 1) if hid % (n * LANES) == 0)
    return pl.pallas_call(
        functools.partial(_post_body, final=final, n_ffn_chunks=n_chunks),
        grid=(ntok // tm,),
        in_specs=[tok(dm), tok(gate.shape[1]), tok(bw), tok(bw), tok(bw),
                  pl.BlockSpec((None, tm, pd), lambda i: (layer, i, 0))]
                 + [_resident(w.shape) for w in weights],
        out_specs=tok(dm),
        out_shape=jax.ShapeDtypeStruct((ntok, dm), F32),
        compiler_params=_cparams(("parallel",)),
        name="post",
    )(h, gate, ya, yb, yc, p3, *weights)


def _layer_weights(i, a_log, dt_bias, w_in, conv_w, norm_mix, gdn_norm, w_pa, ln_v_g, ln_v_b, w_s, b_s,
                   w_pb, w_pc, w_o, norm_ffn, w_ffn_in, w_ffn_out, norm_ple, w_ple_gate, w_ple_proj,
                   dec_seq):
    n_heads = a_log.shape[1]
    c = conv_w.shape[2]
    vw = (c - 2 * n_heads * LANES)
    n_uv = 2 * w_s.shape[1] * LANES
    n_sb = w_pc.shape[1]
    o = 0
    w = w_in[i]
    w_a = w[:, o:o + c + vw]; o += c + vw
    w_ab = w[:, o:o + 2 * n_heads]; o += 2 * n_heads
    w_b = w[:, o:o + n_uv]; o += n_uv
    w_c = w[:, o:o + 3 * n_sb]; o += 3 * n_sb
    w_g = w[:, o:]
    row = lambda x: x.reshape(1, -1).astype(F32)
    lane_pad = lambda x: jnp.pad(x.reshape(1, -1), ((0, 0), (0, LANES - x.size))).astype(F32)
    hid = w_ffn_out.shape[1]
    span = w_s.shape[2]
    rep = ROWS // dec_seq
    lw = dict(
        n_heads=n_heads, n_qkv=c, n_z=vw, n_uv=n_uv, n_sb=n_sb, n_gate=w_g.shape[1],
        norm_mix=row(norm_mix[i]),
        w_a=w_a.astype(BF16), w_ab=jnp.pad(w_ab, ((0, 0), (0, LANES - 2 * n_heads))).astype(BF16),
        w_b=w_b.astype(BF16), w_c=w_c.astype(BF16), w_g=w_g.astype(BF16),
        conv_w=conv_w[i].astype(F32), a_log=lane_pad(a_log[i]), dt_bias=lane_pad(dt_bias[i]),
        gdn_norm=row(gdn_norm[i]),
        ln_v_g=row(ln_v_g[i]), ln_v_b=row(ln_v_b[i]),
        ws_p=w_s[i].astype(F32),
        bs_p=jnp.repeat(b_s[i].T, LANES, axis=1).astype(F32),
        ws_s=jnp.tile(w_s[i][:, :dec_seq, :dec_seq], (1, rep, rep)).astype(F32),
        bs_s=jnp.tile(jnp.repeat(b_s[i][:, :dec_seq].T, LANES, axis=1), (rep, 1)).astype(F32),
        w_pa=w_pa[i].astype(BF16), w_pb=w_pb[i].astype(BF16), w_pc=w_pc[i].astype(BF16),
        w_o=w_o[i].astype(BF16), norm_ffn=row(norm_ffn[i]),
        w_f1=w_ffn_in[i][:, :w_ffn_in.shape[2] // 2].astype(BF16),
        w_f3=w_ffn_in[i][:, w_ffn_in.shape[2] // 2:].astype(BF16),
        w_f2=w_ffn_out[i].astype(BF16), norm_ple=row(norm_ple[i]),
        w_pg=w_ple_gate[i].astype(BF16), w_pp=w_ple_proj[i].astype(BF16),
    )
    assert span == ROWS and hid == lw['w_f1'].shape[1]
    return lw


def _layer(h, p_arr, layer, conv_buf8, s0_arr, s0_index, caches, lw, norm_final, final, batch, seq, tm):
    n_heads = lw['n_heads']
    sample = caches is not None
    aqkv, az, ab, yb, qb, kb, vb, k4, v4, gate, extra = _in_proj(
        h, lw['norm_mix'], lw, tm, batch, seq, conv_buf8)
    pre_act = extra.ndim == 3
    v_rows = None if pre_act else extra
    ya, s_new = _gdn(aqkv, az, ab, conv_buf8, s0_arr, s0_index, lw, batch, seq, pre_act)
    conv_new = extra[:, 5:, :] if pre_act else aqkv.reshape(batch, seq, -1)[:, seq - 3:, :]
    if sample:
        yc = _sb_sample(qb, kb, vb, caches[0], caches[1], layer, batch, seq, n_heads)
    else:
        yc = _sb_prompt(qb, kb, vb, batch, seq, n_heads)
    bw = ya.shape[-1]
    h = _post(h, gate, ya.reshape(-1, bw), yb, yc.reshape(-1, bw), p_arr, layer, lw, norm_final, final, tm)
    kv_shape = (batch, seq, n_heads, LANES)
    return h, conv_new, s_new, k4.reshape(kv_shape), v4.reshape(kv_shape), v_rows


def kernel(x_prompt, x_sample, state_gdn_conv, state_gdn_s, cache_sb_k, cache_sb_v, p_prompt, p_sample, norm_mix, w_in, conv_w, a_log, dt_bias, gdn_norm, w_pa, ln_v_g, ln_v_b, w_s, b_s, w_pb, w_pc, w_o, norm_ffn, w_ffn_in, w_ffn_out, norm_ple, w_ple_gate, w_ple_proj, norm_final):
    depth = w_in.shape[0]
    bp, sp, dm = x_prompt.shape
    bs, ss, _ = x_sample.shape
    n_heads = a_log.shape[1]
    hp = x_prompt.reshape(bp * sp, dm)
    hs = x_sample.reshape(bs * ss, dm)
    nfin = norm_final.reshape(1, dm).astype(F32)
    c = conv_w.shape[2]
    zero_buf = jnp.zeros((bp, 8, c), F32)
    zero_s = jnp.zeros((bp, n_heads, LANES, LANES), F32)
    outs = [[] for _ in range(9)]
    for i in range(depth):
        lw = _layer_weights(i, a_log, dt_bias, w_in, conv_w, norm_mix, gdn_norm, w_pa, ln_v_g, ln_v_b, w_s,
                            b_s, w_pb, w_pc, w_o, norm_ffn, w_ffn_in, w_ffn_out, norm_ple, w_ple_gate,
                            w_ple_proj, ss)
        final = i == depth - 1
        hp, pc, ps, pk, pv, _ = _layer(hp, p_prompt, i, zero_buf, zero_s, lambda b, t: (b, 0, 0, 0), None,
                                       lw, nfin, final, bp, sp, 256)
        buf8 = jnp.pad(state_gdn_conv[i], ((0, 0), (8 - state_gdn_conv.shape[2], 0), (0, 0)))
        hs, sc, sn, sk, sv, sm = _layer(hs, p_sample, i, buf8, state_gdn_s,
                                        lambda b, t, i=i: (i, b, 0, 0, 0), (cache_sb_k, cache_sb_v),
                                        lw, nfin, final, bs, ss, 256)
        for lst, val in zip(outs, (pc, ps, pk, pv, sc, sn, sk, sv, sm.reshape(bs, ss, -1))):
            lst.append(val)
    stacked = [jnp.stack(l) for l in outs]
    return (hp.reshape(bp, sp, dm), hs.reshape(bs, ss, dm), *stacked)
```

```python
import functools
import math

import jax
import jax.numpy as jnp
from jax import lax
from jax.experimental import pallas as pl
from jax.experimental.pallas import tpu as pltpu

F32 = jnp.float32
BF16 = jnp.bfloat16

EPS = 1e-6
LOG2E = 1.4426950408889634
CHUNK = 64
LANES = 128
MXU_WIDTH = 256
ROWS = 128
GDN_TILES = 2
GDN_STREAMS = 2
SB_BLOCK = 256
SB_PROMPT_BLOCK = 256
SB_DEAD_LOG = -104.0
VMEM_LIMIT = 56 * 1024 * 1024


def _cparams(sem):
    return pltpu.CompilerParams(dimension_semantics=sem, vmem_limit_bytes=VMEM_LIMIT)


def _resident(shape):
    nd = len(shape)
    return pl.BlockSpec(shape, lambda *_: (0,) * nd, pipeline_mode=pl.Buffered(1))


def _layer_cols(arr, layer, width=None, start=0):
    width = arr.shape[2] if width is None else width
    assert start % width == 0
    return pl.BlockSpec((None, arr.shape[1], width), lambda *_: (layer, 0, start // width),
                        pipeline_mode=pl.Buffered(1))


def _rms(x, g):
    return x * lax.rsqrt(jnp.mean(x * x, axis=-1, keepdims=True) + EPS) * g


def _dot(a, b):
    return jnp.dot(a.astype(BF16), b.astype(BF16), preferred_element_type=F32)


def _dot_nt(a, b):
    return lax.dot_general(a.astype(BF16), b.astype(BF16), (((1,), (1,)), ((), ())),
                           preferred_element_type=F32)


def _dot_tn(a, b):
    return lax.dot_general(a.astype(BF16), b.astype(BF16), (((0,), (0,)), ((), ())),
                           preferred_element_type=F32)


def _split(x):
    hi = x.astype(BF16)
    lo = (x - hi.astype(F32)).astype(BF16)
    return hi, lo


def _blockdiag(x):
    z = jnp.zeros((x.shape[0], LANES), x.dtype)
    return jnp.concatenate([jnp.concatenate([x[:, :LANES], z], axis=1),
                            jnp.concatenate([z, x[:, LANES:]], axis=1)], axis=0)


def _pdot(a, b):
    return jnp.dot(a.astype(BF16), _blockdiag(b.astype(BF16)), preferred_element_type=F32)


def _pdot3(a, b):
    ah, al = _split(a)
    bh, bl = _split(b)
    bdh, bdl = _blockdiag(bh), _blockdiag(bl)
    d = functools.partial(jnp.dot, preferred_element_type=F32)
    return d(ah, bdh) + (d(al, bdh) + d(ah, bdl))


def _softplus(x):
    return jnp.maximum(x, 0.0) + jnp.log(1.0 + jnp.exp2(jnp.abs(x) * -LOG2E))


def _silu(x):
    return x * jax.nn.sigmoid(x)


def _gmlp_tile(uv, ln_g, ln_b, ws_ref, bs, span):
    half = uv.shape[1] // 2
    ge = 0.5 * uv * (1.0 + lax.erf(uv * (2.0 ** -0.5)))
    u = ge[:, :half]
    v = ge[:, half:]
    mu = jnp.mean(v, axis=-1, keepdims=True)
    vc = v - mu
    vn = vc * lax.rsqrt(jnp.mean(vc * vc, axis=-1, keepdims=True) + EPS) * ln_g + ln_b
    row = lax.broadcasted_iota(jnp.int32, (ROWS, ROWS), 0)
    col = lax.broadcasted_iota(jnp.int32, (ROWS, ROWS), 1)
    mask = ((row & -span) == (col & -span)) & ((row & (span - 1) & -CHUNK) >= (col & (span - 1) & -CHUNK))
    ys = []
    for g in range(half // LANES):
        sl = slice(g * LANES, (g + 1) * LANES)
        w = jnp.where(mask, ws_ref[g], 0.0)
        ys.append(u[:, sl] * (_dot(w, vn[:, sl]) + bs[:, sl]))
    return jnp.concatenate(ys, axis=1), vn


def _in_proj_body(h_ref, g_ref, wa_ref, wab_ref, wb_ref, wc_ref, wg0_ref, wg1_ref, lng_ref, lnb_ref, ws_ref, bs_ref,
                  cbuf_ref, cw_ref,
                  aqkv_ref, az_ref, ab_ref, yb_ref, qb_ref, kb_ref, vb_ref, k4_ref, v4_ref, gate_ref,
                  extra_ref, xprev_sc, *, span, tiles_per_seq):
    xn = _rms(h_ref[...], g_ref[...]).astype(BF16)
    d = functools.partial(jnp.dot, preferred_element_type=F32)
    tm = h_ref.shape[0]
    n_qkv = aqkv_ref.shape[1]
    i = pl.program_id(0)
    if tiles_per_seq:
        @pl.when(i % tiles_per_seq == 0)
        def _():
            xprev_sc[...] = cbuf_ref[...]

    def gdn_stream():
        gdn_heads = (n_qkv - az_ref.shape[1]) // (2 * LANES)
        for kind, sl in _qkv_groups(gdn_heads, n_qkv):
            x = d(xn, wa_ref[:, sl])
            if tiles_per_seq:
                aqkv_ref[:, sl] = _conv_act(x, xprev_sc[:, sl], cw_ref[:, sl], kind)
                xprev_sc[:, sl] = x[tm - 8:, :]
                extra_ref[:, sl] = x[tm - 8:, :]
            else:
                aqkv_ref[:, sl] = x
            yield
        az_ref[...] = d(xn, wa_ref[:, n_qkv:])
        ab_ref[...] = d(xn, wab_ref[...])

    def gmlp_stream():
        buv = d(xn, wb_ref[...])
        yield
        for t in range(tm // ROWS):
            rs = slice(t * ROWS, (t + 1) * ROWS)
            y, vn = _gmlp_tile(buv[rs], lng_ref[...], lnb_ref[...], ws_ref, bs_ref[...], span)
            yb_ref[rs, :] = y.astype(yb_ref.dtype)
            if not tiles_per_seq:
                extra_ref[rs, :] = vn
            yield

    def sb_gate_stream():
        w = qb_ref.shape[1]
        n_heads = w // LANES
        for j, (b_ref, f_ref) in enumerate(((qb_ref, None), (kb_ref, k4_ref), (vb_ref, v4_ref))):
            r = d(xn, wc_ref[:, j * w:(j + 1) * w])
            b_ref[...] = r.astype(BF16)
            if f_ref is not None:
                for h in range(n_heads):
                    f_ref[pl.ds(h, tm, stride=n_heads), :] = r[:, h * LANES:(h + 1) * LANES]
            yield
        step = 4 * LANES
        for part, wg_ref in enumerate((wg0_ref, wg1_ref)):
            gw = wg_ref.shape[1]
            for c0 in range(0, gw, step):
                gate_ref[:, part * gw + c0:part * gw + c0 + step] = d(xn, wg_ref[:, c0:c0 + step])
                yield

    _lockstep([gdn_stream(), gmlp_stream(), sb_gate_stream()])


def _in_proj(h, norm_g, lw, wts, layer, tm, batch, seq, conv_buf8):
    ntok, dm = h.shape
    tm = min(tm, ntok)
    assert tm % ROWS == 0
    tiles_per_seq = seq // tm if seq % tm == 0 else 0
    span = min(seq, ROWS)
    n_heads = lw['n_sb'] // LANES
    half = lw['n_uv'] // 2
    c = lw['n_qkv']
    tok = lambda w, dt=F32: (pl.BlockSpec((tm, w), lambda i: (i, 0)), jax.ShapeDtypeStruct((ntok, w), dt))
    kv4 = (pl.BlockSpec((tm * n_heads, LANES), lambda i: (i, 0)),
           jax.ShapeDtypeStruct((ntok * n_heads, LANES), F32))
    seq_of = (lambda i: (i // tiles_per_seq, 0, 0)) if tiles_per_seq else (lambda i: (0, 0, 0))
    outs = [tok(c), tok(lw['n_z']), tok(LANES), tok(half, BF16),
            tok(lw['n_sb'], BF16), tok(lw['n_sb'], BF16), tok(lw['n_sb'], BF16), kv4, kv4, tok(lw['n_gate'])]
    if tiles_per_seq:
        outs.append((pl.BlockSpec((None, 8, c), seq_of), jax.ShapeDtypeStruct((batch, 8, c), F32)))
    else:
        outs.append(tok(half))
    ws, bs = (lw['ws_p'], lw['bs_p']) if span == ROWS else (lw['ws_s'], lw['bs_s'])
    w_in = wts['w_in']
    n_a, n_b, n_c, n_g = c + lw['n_z'], lw['n_uv'], 3 * lw['n_sb'], lw['n_gate']
    o_b, o_c, o_g, o_ab = n_a, n_a + n_b, n_a + n_b + n_c, n_a + n_b + n_c + n_g
    return pl.pallas_call(
        functools.partial(_in_proj_body, span=span, tiles_per_seq=tiles_per_seq),
        grid=(ntok // tm,),
        in_specs=[tok(dm)[0], _resident((1, dm)), _layer_cols(w_in, layer, n_a, 0),
                  _layer_cols(w_in, layer, LANES, o_ab), _layer_cols(w_in, layer, n_b, o_b),
                  _layer_cols(w_in, layer, n_c, o_c), _layer_cols(w_in, layer, n_g // 2, o_g),
                  _layer_cols(w_in, layer, n_g // 2, o_g + n_g // 2),
                  _resident((1, half)), _resident((1, half)), _resident(ws.shape), _resident(bs.shape),
                  pl.BlockSpec((None, 8, c), seq_of), _resident(lw['conv_w'].shape)],
        out_specs=[o[0] for o in outs],
        out_shape=[o[1] for o in outs],
        scratch_shapes=[pltpu.VMEM((8, c), F32)],
        compiler_params=_cparams(("arbitrary",)),
        name="in_proj",
    )(h, norm_g, w_in, w_in, w_in, w_in, w_in, w_in, lw['ln_v_g'], lw['ln_v_b'], ws, bs,
      conv_buf8, lw['conv_w'])


def _conv_act(x, prev, cw, kind):
    n_tap = cw.shape[0]
    row8 = lax.broadcasted_iota(jnp.int32, (8, x.shape[1]), 0)
    y = x * cw[n_tap - 1:n_tap, :]
    for sh in range(1, n_tap):
        xs = pltpu.roll(x, sh, 0)
        head = jnp.where(row8 < sh, pltpu.roll(prev, sh, 0), xs[:8])
        xs = jnp.concatenate([head, xs[8:]], axis=0)
        y = y + xs * cw[n_tap - 1 - sh:n_tap - sh, :]
    y = _silu(y)
    if kind == 'v':
        return y
    parts = []
    for h in range(x.shape[1] // LANES):
        m = y[:, h * LANES:(h + 1) * LANES]
        m = m * lax.rsqrt(jnp.sum(m * m, axis=-1, keepdims=True) + EPS)
        parts.append(m * (LANES ** -0.5) if kind == 'q' else m)
    return jnp.concatenate(parts, axis=1)


def _qkv_groups(n_heads, c):
    w = n_heads * LANES
    return (('q', slice(0, w)), ('k', slice(w, 2 * w)), ('v', slice(2 * w, c)))


def _qkv_act(x, prev, cw, n_heads):
    return jnp.concatenate([_conv_act(x[:, sl], prev[:, sl], cw[:, sl], kind)
                            for kind, sl in _qkv_groups(n_heads, x.shape[1])], axis=1)


def _gdn_stream(x, prev, ab, z_all, s_pairs, s_next, cw, alog, dtb, gn, *, t_real, n_heads, n_double,
                pre_act):
    R = ROWS
    L2 = 2 * LANES
    qk_w = n_heads * LANES
    row = lax.broadcasted_iota(jnp.int32, (R, LANES), 0)
    valid = row < t_real
    row2 = lax.broadcasted_iota(jnp.int32, (R, L2), 0)
    col2 = lax.broadcasted_iota(jnp.int32, (R, L2), 1) & (LANES - 1)

    y = x if pre_act else _qkv_act(x, prev, cw, n_heads)

    g = -jnp.exp(alog) * _softplus(ab + dtb)
    beta = jax.nn.sigmoid(ab)
    if t_real != R:
        g = jnp.where(valid, g, 0.0)
        beta = jnp.where(valid, beta, 0.0)
    rm = row & (CHUNK - 1)
    gc = g
    sh = 1
    while sh < CHUNK:
        gc = gc + jnp.where(rm >= sh, pltpu.roll(gc, sh, 0), 0.0)
        sh *= 2
    gc_t = gc.T
    e_gc = jnp.exp(gc)
    g_last = jnp.where(row < CHUNK, gc[CHUNK - 1:CHUNK, :], gc[R - 1:R, :])
    e_rest = jnp.exp(g_last - gc)

    same_blk = (row2 & -CHUNK) == (col2 & -CHUNK)
    incl = same_blk & (row2 >= col2)
    strict = same_blk & (row2 > col2)
    eye = (row2 == col2).astype(F32)
    zeros_half = jnp.zeros((CHUNK, L2), F32)

    def cols(m, h0):
        return jnp.concatenate([jnp.broadcast_to(m[:, h0:h0 + 1], (R, LANES)),
                                jnp.broadcast_to(m[:, h0 + 1:h0 + 2], (R, LANES))], axis=1)

    def chain(p):
        h0 = 2 * p
        ps = slice(h0 * LANES, (h0 + 2) * LANES)
        qp = y[:, h0 * LANES:(h0 + 2) * LANES]
        kp = y[:, qk_w + h0 * LANES:qk_w + (h0 + 2) * LANES]
        vp = y[:, 2 * qk_w + h0 * LANES:2 * qk_w + (h0 + 2) * LANES]
        if t_real != R:
            ok = row2 < t_real
            qp, kp, vp = (jnp.where(ok, m, 0.0) for m in (qp, kp, vp))

        gcol = cols(gc, h0)
        grow = jnp.concatenate([gc_t[h0:h0 + 1, :], gc_t[h0 + 1:h0 + 2, :]], axis=1)
        bcol = cols(beta, n_heads + h0)
        ecol = cols(e_gc, h0)
        rcol = cols(e_rest, h0)

        decay = jnp.where(incl, jnp.exp(gcol - grow), 0.0)
        kq = _dot_nt(jnp.concatenate([kp, qp], axis=0), _blockdiag(kp))
        yield
        a_neg = jnp.where(strict, -(bcol * decay * kq[:R]), 0.0)
        p_mat = decay * kq[R:]

        t_inv = eye + a_neg
        pw = _pdot(a_neg, a_neg) if n_double else None
        yield
        for it in range(n_double):
            if it < n_double - 1:
                both = _pdot(jnp.concatenate([pw, t_inv], axis=0), pw)
                pw = both[:R]
                t_inv = t_inv + both[R:]
            else:
                t_inv = t_inv + _pdot(t_inv, pw)
            yield
        resid = (eye - t_inv) + _pdot3(a_neg, t_inv)
        yield
        t_inv = t_inv + _pdot(t_inv, resid)
        yield

        k_in = ecol * kp
        q_dec = ecol * qp
        k_dec = rcol * kp
        while s_pairs[p][0] is None:
            yield
        s_cur = s_pairs[p][0]
        outs = []
        n_live = -(-t_real // CHUNK)
        for c in range(n_live):
            rs = slice(c * CHUNK, (c + 1) * CHUNK)
            ks_qs = _pdot(jnp.concatenate([k_in[rs], q_dec[rs]], axis=0), s_cur)
            yield
            rhs = bcol[rs] * (vp[rs] - ks_qs[:CHUNK])
            parts = [zeros_half] * (R // CHUNK)
            parts[c] = rhs
            u = _pdot3(t_inv[rs, :], jnp.concatenate(parts, axis=0))
            yield
            parts[c] = u
            u_full = jnp.concatenate(parts, axis=0)
            outs.append(ks_qs[CHUNK:] + _pdot(p_mat[rs, :], u_full))
            parts[c] = k_dec[rs]
            kd = jnp.concatenate(parts, axis=0)
            kd = jnp.concatenate([kd[:, :LANES], kd[:, LANES:]], axis=0)
            last = (c + 1) * CHUNK - 1
            bd = jnp.exp(jnp.concatenate([jnp.broadcast_to(gc[last:last + 1, h0:h0 + 1], (1, LANES)),
                                          jnp.broadcast_to(gc[last:last + 1, h0 + 1:h0 + 2], (1, LANES))],
                                         axis=1))
            s_cur = bd * s_cur + _dot_tn(kd, _blockdiag(u_full.astype(BF16)))
            yield
        o = jnp.concatenate(outs, axis=0)
        zg = _silu(z_all[:n_live * CHUNK, ps])
        s_next[p][0] = s_cur
        return jnp.concatenate([_rms(o[:, :LANES], gn), _rms(o[:, LANES:], gn)], axis=1) * zg

    return [chain(p) for p in range(n_heads // 2)]


def _lockstep(gens):
    done = [None] * len(gens)
    live = list(range(len(gens)))
    while live:
        for i in list(live):
            try:
                next(gens[i])
            except StopIteration as stop:
                done[i] = stop.value
                live.remove(i)
    return done


def _gdn_body(qkv_ref, z_ref, ab_ref, cbuf_ref, s0_ref, cw_ref, alog_ref, dtb_ref, gn_ref,
              y_ref, snew_ref, xprev_sc, s_sc, *, t_real, n_heads, n_double, pre_act):
    t = pl.program_id(1)
    nt = pl.num_programs(1)
    n_streams = qkv_ref.shape[0]
    n_pairs = n_heads // 2

    @pl.when(t == 0)
    def _():
        xprev_sc[...] = cbuf_ref[...]
        for s in range(n_streams):
            for p in range(n_pairs):
                s_sc[s, p] = jnp.concatenate([s0_ref[s, 2 * p], s0_ref[s, 2 * p + 1]], axis=1)

    def pad_rows(x):
        if t_real == ROWS:
            return x
        return jnp.concatenate([x, jnp.zeros((ROWS - t_real, x.shape[1]), x.dtype)], axis=0)

    n_tiles = qkv_ref.shape[1] // t_real
    chains, last = [], []
    for s in range(n_streams):
        state = [[s_sc[s, p]] for p in range(n_pairs)]
        for k in range(n_tiles):
            rs = slice(k * t_real, (k + 1) * t_real)
            x = pad_rows(qkv_ref[s, rs, :])
            nxt = [[None] for _ in range(n_pairs)]
            chains += _gdn_stream(x, xprev_sc[s], pad_rows(ab_ref[s, rs, :]), pad_rows(z_ref[s, rs, :]),
                                  state, nxt, cw_ref[...], alog_ref[...], dtb_ref[...], gn_ref[...],
                                  t_real=t_real, n_heads=n_heads, n_double=n_double, pre_act=pre_act)
            state = nxt
            if not pre_act:
                xprev_sc[s] = x[ROWS - 8:, :]
        last.append(state)
    done = iter(_lockstep(chains))
    for s in range(n_streams):
        for k in range(n_tiles):
            for p in range(n_pairs):
                y_ref[s, k * t_real:(k + 1) * t_real, 2 * p * LANES:2 * (p + 1) * LANES] = (
                    next(done)[:t_real].astype(y_ref.dtype))
        for p in range(n_pairs):
            s_sc[s, p] = last[s][p][0]

    @pl.when(t == nt - 1)
    def _():
        for s in range(n_streams):
            for p in range(n_pairs):
                sp = s_sc[s, p]
                snew_ref[s, 2 * p] = sp[:, :LANES]
                snew_ref[s, 2 * p + 1] = sp[:, LANES:]


def _gdn(aqkv, az, ab, conv_buf8, s0_arr, s0_index, lw, batch, seq, pre_act):
    n_heads = lw['n_heads']
    ns = GDN_STREAMS
    assert batch % ns == 0 and n_heads % 2 == 0
    t_real = min(ROWS, seq)
    step_rows = t_real * (GDN_TILES if seq % (ROWS * GDN_TILES) == 0 else 1)
    nt = seq // step_rows
    c = aqkv.shape[1]
    vw = az.shape[1]
    n_double = max(int(math.ceil(math.log2(min(CHUNK, seq)))) - 1, 0)
    body = functools.partial(_gdn_body, t_real=t_real, n_heads=n_heads, n_double=n_double, pre_act=pre_act)
    tok = lambda w: pl.BlockSpec((ns, step_rows, w), lambda b, t: (b, t, 0))
    s_shape = (ns, n_heads, LANES, LANES)
    return pl.pallas_call(
        body,
        grid=(batch // ns, nt),
        in_specs=[tok(c), tok(vw), tok(LANES),
                  pl.BlockSpec((ns, 8, c), lambda b, t: (b, 0, 0)),
                  pl.BlockSpec((None,) * (s0_arr.ndim - 4) + s_shape, s0_index),
                  _resident(lw['conv_w'].shape), _resident((1, LANES)), _resident((1, LANES)),
                  _resident((1, LANES))],
        out_specs=[tok(vw), pl.BlockSpec(s_shape, lambda b, t: (b, 0, 0, 0))],
        out_shape=[jax.ShapeDtypeStruct((batch, seq, vw), BF16),
                   jax.ShapeDtypeStruct((batch,) + s_shape[1:], F32)],
        scratch_shapes=[pltpu.VMEM((ns, 8, c), F32), pltpu.VMEM((ns, n_heads // 2, LANES, 2 * LANES), F32)],
        compiler_params=_cparams(("parallel", "arbitrary")),
        name="gdn",
    )(aqkv.reshape(batch, seq, c), az.reshape(batch, seq, vw), ab.reshape(batch, seq, LANES),
      conv_buf8, s0_arr, lw['conv_w'], lw['a_log'], lw['dt_bias'], lw['gdn_norm'])


def _sb_block_stages(q, kb, vb, carry, tri, mask):
    z = _dot_nt(q, kb) * (LANES ** -0.5)
    yield
    sp = _softplus(z)
    lf = -sp if mask is None else jnp.where(mask, -sp, 0.0)
    hi, lo = _split(lf)
    after = (jnp.dot(hi, tri, preferred_element_type=F32)
             + jnp.dot(lo, tri, preferred_element_type=F32))
    yield
    a = jnp.exp((z - sp) + after)
    if mask is not None:
        a = jnp.where(mask, a, 0.0)
    out = jnp.exp(carry) * _dot(a, vb)
    return out, jnp.sum(lf, axis=-1, keepdims=True)


def _sb_block(q, kb, vb, carry, tri, mask):
    return _lockstep([_sb_block_stages(q, kb, vb, carry, tri, mask)])[0]


def _sb_prompt_body(q_ref, k_ref, v_ref, o_ref, acc_sc, car_sc, *, n_heads):
    i = pl.program_id(1)
    tq = q_ref.shape[0]
    row = lax.broadcasted_iota(jnp.int32, (tq, tq), 0)
    col = lax.broadcasted_iota(jnp.int32, (tq, tq), 1)
    tri = (row > col).astype(BF16)
    heads = [slice(h * LANES, (h + 1) * LANES) for h in range(n_heads)]
    qs = [q_ref[:, sl] for sl in heads]

    def blocks(off, carries, mask):
        kb = k_ref[pl.ds(off, tq), :]
        vb = v_ref[pl.ds(off, tq), :]
        return _lockstep([_sb_block_stages(qs[h], kb[:, sl], vb[:, sl], carries[h], tri, mask)
                          for h, sl in enumerate(heads)])

    zero = jnp.zeros((tq, 1), F32)
    res = blocks(pl.multiple_of(i * tq, tq), [zero] * n_heads, col < row)
    live = None
    for h, (out, csum) in enumerate(res):
        acc_sc[h] = out
        car_sc[h] = csum
        m = jnp.max(csum)
        live = m if live is None else jnp.maximum(live, m)

    def cond(st):
        j, live = st
        return (j >= 0) & (live > SB_DEAD_LOG)

    def body(st):
        j, _ = st
        carries = [car_sc[h] for h in range(n_heads)]
        res = blocks(pl.multiple_of(j * tq, tq), carries, None)
        live = None
        for h, (out, csum) in enumerate(res):
            acc_sc[h] += out
            carry = carries[h] + csum
            car_sc[h] = carry
            m = jnp.max(carry)
            live = m if live is None else jnp.maximum(live, m)
        return j - 1, live

    lax.while_loop(cond, body, (i - 1, live))
    for h, sl in enumerate(heads):
        o_ref[:, sl] = acc_sc[h].astype(o_ref.dtype)


def _sb_prompt(q, k, v, batch, seq, n_heads):
    tq = min(SB_PROMPT_BLOCK, seq)
    w = n_heads * LANES
    blk = pl.BlockSpec((None, tq, w), lambda b, i: (b, i, 0))
    full = pl.BlockSpec((None, seq, w), lambda b, i: (b, 0, 0), pipeline_mode=pl.Buffered(1))
    return pl.pallas_call(
        functools.partial(_sb_prompt_body, n_heads=n_heads),
        grid=(batch, seq // tq),
        in_specs=[blk, full, full],
        out_specs=blk,
        out_shape=jax.ShapeDtypeStruct((batch, seq, w), BF16),
        scratch_shapes=[pltpu.VMEM((n_heads, tq, LANES), F32), pltpu.VMEM((n_heads, tq, 1), F32)],
        compiler_params=_cparams(("parallel", "arbitrary")),
        name="sb_prompt",
    )(q.reshape(batch, seq, w), k.reshape(batch, seq, w), v.reshape(batch, seq, w))


def _sb_sample_body(q_ref, k_ref, v_ref, kp_hbm, vp_hbm, o_ref, kbuf, vbuf, sem, acc_sc, car_sc,
                    *, n_heads, tk, layer):
    b = pl.program_id(0)
    n_streams = pl.num_programs(0)
    tq = q_ref.shape[0]
    blk_rows = tk * n_heads
    n_blk = kp_hbm.shape[2] // blk_rows
    newest = n_blk - 1

    def copies(stream, j, slot):
        src = pl.ds(j * blk_rows, blk_rows)
        return (pltpu.make_async_copy(kp_hbm.at[layer, stream, src, :], kbuf.at[slot], sem.at[0, slot]),
                pltpu.make_async_copy(vp_hbm.at[layer, stream, src, :], vbuf.at[slot], sem.at[1, slot]))

    def start(stream, j, slot):
        for c in copies(stream, j, slot):
            c.start()

    def wait(stream, j, slot):
        for c in copies(stream, j, slot):
            c.wait()

    def slot_of(j):
        return jnp.where(j == newest, 2 + lax.rem(b, 2), lax.rem(newest - 1 - j, 2))

    @pl.when(b == 0)
    def _():
        start(b, newest, 2)

    @pl.when(b + 1 < n_streams)
    def _():
        start(b + 1, newest, 2 + lax.rem(b + 1, 2))

    if n_blk > 1:
        start(b, newest - 1, 0)

    row = lax.broadcasted_iota(jnp.int32, (tq, LANES), 0)
    col = lax.broadcasted_iota(jnp.int32, (tq, LANES), 1)
    r2 = lax.broadcasted_iota(jnp.int32, (LANES, LANES), 0)
    c2 = lax.broadcasted_iota(jnp.int32, (LANES, LANES), 1)
    tri_new = (r2 > c2).astype(BF16)
    r3 = lax.broadcasted_iota(jnp.int32, (tk, tk), 0)
    c3 = lax.broadcasted_iota(jnp.int32, (tk, tk), 1)
    tri_past = (r3 > c3).astype(BF16)
    pad = jnp.zeros((LANES - tq, LANES), BF16)
    heads = [slice(h * LANES, (h + 1) * LANES) for h in range(n_heads)]
    qs = [q_ref[:, sl] for sl in heads]

    def fold(res, carries, first):
        live = None
        for h, (out, csum) in enumerate(res):
            if first:
                acc_sc[h] = out
                carry = csum
            else:
                acc_sc[h] += out
                carry = carries[h] + csum
            car_sc[h] = carry
            m = jnp.max(carry)
            live = m if live is None else jnp.maximum(live, m)
        return live

    zero = jnp.zeros((tq, 1), F32)
    live = fold(_lockstep([_sb_block_stages(qs[h], jnp.concatenate([k_ref[:, sl], pad], axis=0),
                                            jnp.concatenate([v_ref[:, sl], pad], axis=0),
                                            zero, tri_new, col < row)
                           for h, sl in enumerate(heads)]), None, True)

    def cond(st):
        j, live = st
        return (j >= 0) & (live > SB_DEAD_LOG)

    def body(st):
        j, _ = st
        slot = slot_of(j)
        wait(b, j, slot)

        @pl.when((j > 0) & (j < newest))
        def _():
            start(b, j - 1, slot_of(j - 1))

        carries = [car_sc[h] for h in range(n_heads)]
        res = _lockstep([_sb_block_stages(qs[h], kbuf[slot, pl.ds(h, tk, stride=n_heads), :],
                                          vbuf[slot, pl.ds(h, tk, stride=n_heads), :],
                                          carries[h], tri_past, None) for h in range(n_heads)])
        return j - 1, fold(res, carries, False)

    j_end, _ = lax.while_loop(cond, body, (newest, live))

    @pl.when(j_end == newest)
    def _():
        wait(b, newest, slot_of(newest))

    if n_blk > 1:
        @pl.when(j_end >= 0)
        def _():
            ahead = jnp.minimum(j_end, newest - 1)
            wait(b, ahead, slot_of(ahead))

    for h, sl in enumerate(heads):
        o_ref[:, sl] = acc_sc[h].astype(o_ref.dtype)


def _sb_sample(q, k, v, cache_k, cache_v, layer, batch, seq, n_heads):
    w = n_heads * LANES
    depth, _, past = cache_k.shape[:3]
    tk = min(SB_BLOCK, past)
    assert past % tk == 0
    new = pl.BlockSpec((None, seq, w), lambda b: (b, 0, 0))
    hbm = pl.BlockSpec(memory_space=pl.ANY)
    rows = lambda c: c.reshape(depth, batch, past * n_heads, LANES)
    return pl.pallas_call(
        functools.partial(_sb_sample_body, n_heads=n_heads, tk=tk, layer=layer),
        grid=(batch,),
        in_specs=[new, new, new, hbm, hbm],
        out_specs=new,
        out_shape=jax.ShapeDtypeStruct((batch, seq, w), BF16),
        scratch_shapes=[pltpu.VMEM((4, tk * n_heads, LANES), F32), pltpu.VMEM((4, tk * n_heads, LANES), F32),
                        pltpu.SemaphoreType.DMA((2, 4)),
                        pltpu.VMEM((n_heads, seq, LANES), F32), pltpu.VMEM((n_heads, seq, 1), F32)],
        compiler_params=_cparams(("arbitrary",)),
        name="sb_sample",
    )(q.reshape(batch, seq, w), k.reshape(batch, seq, w), v.reshape(batch, seq, w), rows(cache_k), rows(cache_v))


def _post_body(h_ref, gate_ref, ya_ref, yb_ref, yc_ref, p_ref,
               wpa_ref, wpb_ref, wpc_ref, wo_ref, nf_ref, wf1_ref, wf3_ref, wf2_ref,
               npl_ref, wpg_ref, wpp_ref, nfin_ref, o_ref, *, final, ffn_bounds):
    d = functools.partial(jnp.dot, preferred_element_type=F32)
    h = h_ref[...]
    dm = h.shape[1]
    gates = jax.nn.sigmoid(gate_ref[...])
    merged = (gates[:, :dm] * d(ya_ref[...], wpa_ref[...])
              + gates[:, dm:2 * dm] * d(yb_ref[...], wpb_ref[...])
              + gates[:, 2 * dm:] * d(yc_ref[...], wpc_ref[...]))
    h = h + d(merged.astype(BF16), wo_ref[...])
    xn = _rms(h, nf_ref[...]).astype(BF16)
    ff = None
    for lo, hi in zip(ffn_bounds[:-1], ffn_bounds[1:]):
        cs = slice(lo, hi)
        a1 = d(xn, wf1_ref[:, cs])
        a3 = d(xn, wf3_ref[:, cs])
        part = d((_silu(a1) * a3).astype(BF16), wf2_ref[cs, :])
        ff = part if ff is None else ff + part
    h = h + ff
    xn = _rms(h, npl_ref[...]).astype(BF16)
    h = h + jax.nn.sigmoid(d(xn, wpg_ref[...])) * d(p_ref[...].astype(BF16), wpp_ref[...])
    if final:
        h = _rms(h, nfin_ref[...])
    o_ref[...] = h


def _post(h, gate, ya, yb, yc, p_arr, layer, lw, wts, norm_final, final, tm):
    ntok, dm = h.shape
    tm = min(tm, ntok)
    bw = ya.shape[1]
    pd = p_arr.shape[-1]
    p3 = p_arr.reshape(p_arr.shape[0], ntok, pd)
    tok = lambda w: pl.BlockSpec((tm, w), lambda i: (i, 0))
    hid = wts['w_ffn_out'].shape[1]
    whole = lambda name: (_layer_cols(wts[name], layer), wts[name])
    vec = lambda x: (_resident(x.shape), x)
    weights = [whole('w_pa'), whole('w_pb'), whole('w_pc'), whole('w_o'), vec(lw['norm_ffn']),
               (_layer_cols(wts['w_ffn_in'], layer, hid, 0), wts['w_ffn_in']),
               (_layer_cols(wts['w_ffn_in'], layer, hid, hid), wts['w_ffn_in']),
               whole('w_ffn_out'), vec(lw['norm_ple']), whole('w_ple_gate'), whole('w_ple_proj'),
               vec(norm_final)]
    mid = -(-hid // (2 * MXU_WIDTH)) * MXU_WIDTH if hid % MXU_WIDTH == 0 else hid // 2
    return pl.pallas_call(
        functools.partial(_post_body, final=final, ffn_bounds=(0, mid, hid)),
        grid=(ntok // tm,),
        in_specs=[tok(dm), tok(gate.shape[1]), tok(bw), tok(bw), tok(bw),
                  pl.BlockSpec((None, tm, pd), lambda i: (layer, i, 0))]
                 + [w[0] for w in weights],
        out_specs=tok(dm),
        out_shape=jax.ShapeDtypeStruct((ntok, dm), F32),
        compiler_params=_cparams(("parallel",)),
        name="post",
    )(h, gate, ya, yb, yc, p3, *[w[1] for w in weights])


def _pack_weights(n_heads, n_a, w_in, w_pa, w_pb, w_pc, w_o, w_ffn_in, w_ffn_out, w_ple_gate, w_ple_proj):
    n_ab = 2 * n_heads
    w_in_p = jnp.concatenate([w_in[:, :, :n_a], w_in[:, :, n_a + n_ab:],
                              jnp.pad(w_in[:, :, n_a:n_a + n_ab], ((0, 0), (0, 0), (0, LANES - n_ab)))],
                             axis=2).astype(BF16)
    cast = lambda w: w.astype(BF16)
    return dict(w_in=w_in_p, w_pa=cast(w_pa), w_pb=cast(w_pb), w_pc=cast(w_pc), w_o=cast(w_o),
                w_ffn_in=cast(w_ffn_in), w_ffn_out=cast(w_ffn_out), w_ple_gate=cast(w_ple_gate),
                w_ple_proj=cast(w_ple_proj))


def _layer_weights(i, a_log, dt_bias, w_in, conv_w, norm_mix, gdn_norm, ln_v_g, ln_v_b, w_s, b_s, w_pc,
                   norm_ffn, norm_ple, dec_seq):
    n_heads = a_log.shape[1]
    c = conv_w.shape[2]
    vw = (c - 2 * n_heads * LANES)
    n_uv = 2 * w_s.shape[1] * LANES
    n_sb = w_pc.shape[1]
    n_gate = w_in.shape[2] - (c + vw) - 2 * n_heads - n_uv - 3 * n_sb
    row = lambda x: x.reshape(1, -1).astype(F32)
    lane_pad = lambda x: jnp.pad(x.reshape(1, -1), ((0, 0), (0, LANES - x.size))).astype(F32)
    span = w_s.shape[2]
    rep = ROWS // dec_seq
    lw = dict(
        n_heads=n_heads, n_qkv=c, n_z=vw, n_uv=n_uv, n_sb=n_sb, n_gate=n_gate,
        norm_mix=row(norm_mix[i]),
        conv_w=conv_w[i].astype(F32), a_log=lane_pad(a_log[i]), dt_bias=lane_pad(dt_bias[i]),
        gdn_norm=row(gdn_norm[i]),
        ln_v_g=row(ln_v_g[i]), ln_v_b=row(ln_v_b[i]),
        ws_p=w_s[i].astype(F32),
        bs_p=jnp.repeat(b_s[i].T, LANES, axis=1).astype(F32),
        ws_s=jnp.tile(w_s[i][:, :dec_seq, :dec_seq], (1, rep, rep)).astype(F32),
        bs_s=jnp.tile(jnp.repeat(b_s[i][:, :dec_seq].T, LANES, axis=1), (rep, 1)).astype(F32),
        norm_ffn=row(norm_ffn[i]), norm_ple=row(norm_ple[i]),
    )
    assert span == ROWS
    return lw


def _layer(h, p_arr, layer, conv_buf8, s0_arr, s0_index, caches, lw, wts, norm_final, final, batch, seq, tm):
    n_heads = lw['n_heads']
    sample = caches is not None
    aqkv, az, ab, yb, qb, kb, vb, k4, v4, gate, extra = _in_proj(
        h, lw['norm_mix'], lw, wts, layer, tm, batch, seq, conv_buf8)
    pre_act = extra.ndim == 3
    v_rows = None if pre_act else extra
    ya, s_new = _gdn(aqkv, az, ab, conv_buf8, s0_arr, s0_index, lw, batch, seq, pre_act)
    conv_new = extra[:, 5:, :] if pre_act else aqkv.reshape(batch, seq, -1)[:, seq - 3:, :]
    if sample:
        yc = _sb_sample(qb, kb, vb, caches[0], caches[1], layer, batch, seq, n_heads)
    else:
        yc = _sb_prompt(qb, kb, vb, batch, seq, n_heads)
    bw = ya.shape[-1]
    h = _post(h, gate, ya.reshape(-1, bw), yb, yc.reshape(-1, bw), p_arr, layer, lw, wts, norm_final, final,
              tm)
    kv_shape = (batch, seq, n_heads, LANES)
    return h, conv_new, s_new, k4.reshape(kv_shape), v4.reshape(kv_shape), v_rows


def kernel(x_prompt, x_sample, state_gdn_conv, state_gdn_s, cache_sb_k, cache_sb_v, p_prompt, p_sample, norm_mix, w_in, conv_w, a_log, dt_bias, gdn_norm, w_pa, ln_v_g, ln_v_b, w_s, b_s, w_pb, w_pc, w_o, norm_ffn, w_ffn_in, w_ffn_out, norm_ple, w_ple_gate, w_ple_proj, norm_final):
    depth = w_in.shape[0]
    bp, sp, dm = x_prompt.shape
    bs, ss, _ = x_sample.shape
    n_heads = a_log.shape[1]
    hp = x_prompt.reshape(bp * sp, dm)
    hs = x_sample.reshape(bs * ss, dm)
    nfin = norm_final.reshape(1, dm).astype(F32)
    c = conv_w.shape[2]
    zero_buf = jnp.zeros((bp, 8, c), F32)
    zero_s = jnp.zeros((bp, n_heads, LANES, LANES), F32)
    outs = [[] for _ in range(9)]
    wts = _pack_weights(n_heads, c + (c - 2 * n_heads * LANES), w_in, w_pa, w_pb, w_pc, w_o, w_ffn_in,
                        w_ffn_out, w_ple_gate, w_ple_proj)
    for i in range(depth):
        lw = _layer_weights(i, a_log, dt_bias, w_in, conv_w, norm_mix, gdn_norm, ln_v_g, ln_v_b, w_s, b_s, w_pc,
                            norm_ffn, norm_ple, ss)
        final = i == depth - 1
        hp, pc, ps, pk, pv, _ = _layer(hp, p_prompt, i, zero_buf, zero_s, lambda b, t: (b, 0, 0, 0), None,
                                       lw, wts, nfin, final, bp, sp, 256)
        buf8 = jnp.pad(state_gdn_conv[i], ((0, 0), (8 - state_gdn_conv.shape[2], 0), (0, 0)))
        hs, sc, sn, sk, sv, sm = _layer(hs, p_sample, i, buf8, state_gdn_s,
                                        lambda b, t, i=i: (i, b, 0, 0, 0), (cache_sb_k, cache_sb_v),
                                        lw, wts, nfin, final, bs, ss, 256)
        for lst, val in zip(outs, (pc, ps, pk, pv, sc, sn, sk, sv, sm.reshape(bs, ss, -1))):
            lst.append(val)
    stacked = [jnp.stack(l) for l in outs]
    return (hp.reshape(bp, sp, dm), hs.reshape(bs, ss, dm), *stacked)
```

```python
import functools
import math

import jax
import jax.numpy as jnp
from jax import lax
from jax.experimental import pallas as pl
from jax.experimental.pallas import tpu as pltpu

F32 = jnp.float32
BF16 = jnp.bfloat16

EPS = 1e-6
LOG2E = 1.4426950408889634
CHUNK = 64
LANES = 128
MXU_WIDTH = 256
ROWS = 128
GDN_TILES = 2
GDN_STREAMS = 2
SB_BLOCK = 256
SB_PROMPT_BLOCK = 256
SB_DEAD_LOG = -104.0
VMEM_LIMIT = 56 * 1024 * 1024


def _cparams(sem):
    return pltpu.CompilerParams(dimension_semantics=sem, vmem_limit_bytes=VMEM_LIMIT)


def _resident(shape):
    nd = len(shape)
    return pl.BlockSpec(shape, lambda *_: (0,) * nd, pipeline_mode=pl.Buffered(1))


def _layer_cols(arr, layer, width=None, start=0):
    width = arr.shape[2] if width is None else width
    assert start % width == 0
    return pl.BlockSpec((None, arr.shape[1], width), lambda *_: (layer, 0, start // width),
                        pipeline_mode=pl.Buffered(1))


def _rms(x, g):
    return x * lax.rsqrt(jnp.mean(x * x, axis=-1, keepdims=True) + EPS) * g


def _dot(a, b):
    return jnp.dot(a.astype(BF16), b.astype(BF16), preferred_element_type=F32)


def _dot_nt(a, b):
    return lax.dot_general(a.astype(BF16), b.astype(BF16), (((1,), (1,)), ((), ())),
                           preferred_element_type=F32)


def _dot_tn(a, b):
    return lax.dot_general(a.astype(BF16), b.astype(BF16), (((0,), (0,)), ((), ())),
                           preferred_element_type=F32)


def _split(x):
    hi = x.astype(BF16)
    lo = (x - hi.astype(F32)).astype(BF16)
    return hi, lo


def _blockdiag(x):
    z = jnp.zeros((x.shape[0], LANES), x.dtype)
    return jnp.concatenate([jnp.concatenate([x[:, :LANES], z], axis=1),
                            jnp.concatenate([z, x[:, LANES:]], axis=1)], axis=0)


def _pdot(a, b):
    return jnp.dot(a.astype(BF16), _blockdiag(b.astype(BF16)), preferred_element_type=F32)


def _pdot3(a, b):
    ah, al = _split(a)
    bh, bl = _split(b)
    bdh, bdl = _blockdiag(bh), _blockdiag(bl)
    d = functools.partial(jnp.dot, preferred_element_type=F32)
    return d(ah, bdh) + (d(al, bdh) + d(ah, bdl))


def _softplus(x):
    return jnp.maximum(x, 0.0) + jnp.log(1.0 + jnp.exp2(jnp.abs(x) * -LOG2E))


def _silu(x):
    return x * jax.nn.sigmoid(x)


def _gmlp_tile(uv, ln_g, ln_b, ws_ref, bs, span):
    half = uv.shape[1] // 2
    ge = 0.5 * uv * (1.0 + lax.erf(uv * (2.0 ** -0.5)))
    u = ge[:, :half]
    v = ge[:, half:]
    mu = jnp.mean(v, axis=-1, keepdims=True)
    vc = v - mu
    vn = vc * lax.rsqrt(jnp.mean(vc * vc, axis=-1, keepdims=True) + EPS) * ln_g + ln_b
    row = lax.broadcasted_iota(jnp.int32, (ROWS, ROWS), 0)
    col = lax.broadcasted_iota(jnp.int32, (ROWS, ROWS), 1)
    mask = ((row & -span) == (col & -span)) & ((row & (span - 1) & -CHUNK) >= (col & (span - 1) & -CHUNK))
    ys = []
    for g in range(half // LANES):
        sl = slice(g * LANES, (g + 1) * LANES)
        w = jnp.where(mask, ws_ref[g], 0.0)
        ys.append(u[:, sl] * (_dot(w, vn[:, sl]) + bs[:, sl]))
    return jnp.concatenate(ys, axis=1), vn


def _in_proj_body(h_ref, g_ref, wa_ref, wab_ref, wb_ref, wc_ref, wg0_ref, wg1_ref, lng_ref, lnb_ref, ws_ref, bs_ref,
                  cbuf_ref, cw_ref,
                  aqkv_ref, az_ref, ab_ref, yb_ref, qb_ref, kb_ref, vb_ref, k4_ref, v4_ref, gate_ref,
                  extra_ref, xprev_sc, *, span, tiles_per_seq):
    xn = _rms(h_ref[...], g_ref[...]).astype(BF16)
    d = functools.partial(jnp.dot, preferred_element_type=F32)
    tm = h_ref.shape[0]
    n_qkv = aqkv_ref.shape[1]
    i = pl.program_id(0)
    if tiles_per_seq:
        @pl.when(i % tiles_per_seq == 0)
        def _():
            xprev_sc[...] = cbuf_ref[...]

    def gdn_stream():
        gdn_heads = (n_qkv - az_ref.shape[1]) // (2 * LANES)
        for kind, sl in _qkv_groups(gdn_heads, n_qkv):
            x = d(xn, wa_ref[:, sl])
            if tiles_per_seq:
                aqkv_ref[:, sl] = _conv_act(x, xprev_sc[:, sl], cw_ref[:, sl], kind)
                xprev_sc[:, sl] = x[tm - 8:, :]
                extra_ref[:, sl] = x[tm - 8:, :]
            else:
                aqkv_ref[:, sl] = x
            yield
        az_ref[...] = d(xn, wa_ref[:, n_qkv:])
        ab_ref[...] = d(xn, wab_ref[...])

    def gmlp_stream():
        buv = d(xn, wb_ref[...])
        yield
        for t in range(tm // ROWS):
            rs = slice(t * ROWS, (t + 1) * ROWS)
            y, vn = _gmlp_tile(buv[rs], lng_ref[...], lnb_ref[...], ws_ref, bs_ref[...], span)
            yb_ref[rs, :] = y.astype(yb_ref.dtype)
            if not tiles_per_seq:
                extra_ref[rs, :] = vn
            yield

    def sb_gate_stream():
        w = qb_ref.shape[1]
        n_heads = w // LANES
        for j, (b_ref, f_ref) in enumerate(((qb_ref, None), (kb_ref, k4_ref), (vb_ref, v4_ref))):
            r = d(xn, wc_ref[:, j * w:(j + 1) * w])
            b_ref[...] = r.astype(BF16)
            if f_ref is not None:
                for h in range(n_heads):
                    f_ref[pl.ds(h, tm, stride=n_heads), :] = r[:, h * LANES:(h + 1) * LANES]
            yield
        step = 4 * LANES
        for part, wg_ref in enumerate((wg0_ref, wg1_ref)):
            gw = wg_ref.shape[1]
            for c0 in range(0, gw, step):
                gate_ref[:, part * gw + c0:part * gw + c0 + step] = d(xn, wg_ref[:, c0:c0 + step])
                yield

    _lockstep([gdn_stream(), gmlp_stream(), sb_gate_stream()])


def _in_proj(h, norm_g, lw, wts, layer, tm, batch, seq, conv_buf8):
    ntok, dm = h.shape
    tm = min(tm, ntok)
    assert tm % ROWS == 0
    tiles_per_seq = seq // tm if seq % tm == 0 else 0
    span = min(seq, ROWS)
    n_heads = lw['n_sb'] // LANES
    half = lw['n_uv'] // 2
    c = lw['n_qkv']
    tok = lambda w, dt=F32: (pl.BlockSpec((tm, w), lambda i: (i, 0)), jax.ShapeDtypeStruct((ntok, w), dt))
    kv4 = (pl.BlockSpec((tm * n_heads, LANES), lambda i: (i, 0)),
           jax.ShapeDtypeStruct((ntok * n_heads, LANES), F32))
    seq_of = (lambda i: (i // tiles_per_seq, 0, 0)) if tiles_per_seq else (lambda i: (0, 0, 0))
    outs = [tok(c), tok(lw['n_z']), tok(LANES), tok(half, BF16),
            tok(lw['n_sb'], BF16), tok(lw['n_sb'], BF16), tok(lw['n_sb'], BF16), kv4, kv4, tok(lw['n_gate'])]
    if tiles_per_seq:
        outs.append((pl.BlockSpec((None, 8, c), seq_of), jax.ShapeDtypeStruct((batch, 8, c), F32)))
    else:
        outs.append(tok(half))
    ws, bs = (lw['ws_p'], lw['bs_p']) if span == ROWS else (lw['ws_s'], lw['bs_s'])
    w_in = wts['w_in']
    n_a, n_b, n_c, n_g = c + lw['n_z'], lw['n_uv'], 3 * lw['n_sb'], lw['n_gate']
    o_b, o_c, o_g, o_ab = n_a, n_a + n_b, n_a + n_b + n_c, n_a + n_b + n_c + n_g
    return pl.pallas_call(
        functools.partial(_in_proj_body, span=span, tiles_per_seq=tiles_per_seq),
        grid=(ntok // tm,),
        in_specs=[tok(dm)[0], _resident((1, dm)), _layer_cols(w_in, layer, n_a, 0),
                  _layer_cols(w_in, layer, LANES, o_ab), _layer_cols(w_in, layer, n_b, o_b),
                  _layer_cols(w_in, layer, n_c, o_c), _layer_cols(w_in, layer, n_g // 2, o_g),
                  _layer_cols(w_in, layer, n_g // 2, o_g + n_g // 2),
                  _resident((1, half)), _resident((1, half)), _resident(ws.shape), _resident(bs.shape),
                  pl.BlockSpec((None, 8, c), seq_of), _resident(lw['conv_w'].shape)],
        out_specs=[o[0] for o in outs],
        out_shape=[o[1] for o in outs],
        scratch_shapes=[pltpu.VMEM((8, c), F32)],
        compiler_params=_cparams(("arbitrary",)),
        name="in_proj",
    )(h, norm_g, w_in, w_in, w_in, w_in, w_in, w_in, lw['ln_v_g'], lw['ln_v_b'], ws, bs,
      conv_buf8, lw['conv_w'])


def _conv_act(x, prev, cw, kind):
    n_tap = cw.shape[0]
    row8 = lax.broadcasted_iota(jnp.int32, (8, x.shape[1]), 0)
    y = x * cw[n_tap - 1:n_tap, :]
    for sh in range(1, n_tap):
        xs = pltpu.roll(x, sh, 0)
        head = jnp.where(row8 < sh, pltpu.roll(prev, sh, 0), xs[:8])
        xs = jnp.concatenate([head, xs[8:]], axis=0)
        y = y + xs * cw[n_tap - 1 - sh:n_tap - sh, :]
    y = _silu(y)
    if kind == 'v':
        return y
    parts = []
    for h in range(x.shape[1] // LANES):
        m = y[:, h * LANES:(h + 1) * LANES]
        m = m * lax.rsqrt(jnp.sum(m * m, axis=-1, keepdims=True) + EPS)
        parts.append(m * (LANES ** -0.5) if kind == 'q' else m)
    return jnp.concatenate(parts, axis=1)


def _qkv_groups(n_heads, c):
    w = n_heads * LANES
    return (('q', slice(0, w)), ('k', slice(w, 2 * w)), ('v', slice(2 * w, c)))


def _qkv_act(x, prev, cw, n_heads):
    return jnp.concatenate([_conv_act(x[:, sl], prev[:, sl], cw[:, sl], kind)
                            for kind, sl in _qkv_groups(n_heads, x.shape[1])], axis=1)


def _gdn_stream(x, prev, ab, z_all, s_pairs, s_next, cw, alog, dtb, gn, *, t_real, n_heads, n_double,
                pre_act):
    R = ROWS
    L2 = 2 * LANES
    qk_w = n_heads * LANES
    row = lax.broadcasted_iota(jnp.int32, (R, LANES), 0)
    valid = row < t_real
    row2 = lax.broadcasted_iota(jnp.int32, (R, L2), 0)
    col2 = lax.broadcasted_iota(jnp.int32, (R, L2), 1) & (LANES - 1)

    y = x if pre_act else _qkv_act(x, prev, cw, n_heads)

    g = -jnp.exp(alog) * _softplus(ab + dtb)
    beta = jax.nn.sigmoid(ab)
    if t_real != R:
        g = jnp.where(valid, g, 0.0)
        beta = jnp.where(valid, beta, 0.0)
    rm = row & (CHUNK - 1)
    gc = g
    sh = 1
    while sh < CHUNK:
        gc = gc + jnp.where(rm >= sh, pltpu.roll(gc, sh, 0), 0.0)
        sh *= 2
    gc_t = gc.T
    e_gc = jnp.exp(gc)
    g_last = jnp.where(row < CHUNK, gc[CHUNK - 1:CHUNK, :], gc[R - 1:R, :])
    e_rest = jnp.exp(g_last - gc)

    same_blk = (row2 & -CHUNK) == (col2 & -CHUNK)
    incl = same_blk & (row2 >= col2)
    strict = same_blk & (row2 > col2)
    eye = (row2 == col2).astype(F32)
    zeros_half = jnp.zeros((CHUNK, L2), F32)

    def cols(m, h0):
        return jnp.concatenate([jnp.broadcast_to(m[:, h0:h0 + 1], (R, LANES)),
                                jnp.broadcast_to(m[:, h0 + 1:h0 + 2], (R, LANES))], axis=1)

    def chain(p):
        h0 = 2 * p
        ps = slice(h0 * LANES, (h0 + 2) * LANES)
        qp = y[:, h0 * LANES:(h0 + 2) * LANES]
        kp = y[:, qk_w + h0 * LANES:qk_w + (h0 + 2) * LANES]
        vp = y[:, 2 * qk_w + h0 * LANES:2 * qk_w + (h0 + 2) * LANES]
        if t_real != R:
            ok = row2 < t_real
            qp, kp, vp = (jnp.where(ok, m, 0.0) for m in (qp, kp, vp))

        gcol = cols(gc, h0)
        grow = jnp.concatenate([gc_t[h0:h0 + 1, :], gc_t[h0 + 1:h0 + 2, :]], axis=1)
        bcol = cols(beta, n_heads + h0)
        ecol = cols(e_gc, h0)
        rcol = cols(e_rest, h0)

        decay = jnp.where(incl, jnp.exp(gcol - grow), 0.0)
        kq = _dot_nt(jnp.concatenate([kp, qp], axis=0), _blockdiag(kp))
        yield
        a_neg = jnp.where(strict, -(bcol * decay * kq[:R]), 0.0)
        p_mat = decay * kq[R:]

        t_inv = eye + a_neg
        pw = _pdot(a_neg, a_neg) if n_double else None
        yield
        for it in range(n_double):
            if it < n_double - 1:
                both = _pdot(jnp.concatenate([pw, t_inv], axis=0), pw)
                pw = both[:R]
                t_inv = t_inv + both[R:]
            else:
                t_inv = t_inv + _pdot(t_inv, pw)
            yield
        resid = (eye - t_inv) + _pdot3(a_neg, t_inv)
        yield
        t_inv = t_inv + _pdot(t_inv, resid)
        yield

        k_in = ecol * kp
        q_dec = ecol * qp
        k_dec = rcol * kp
        while s_pairs[p][0] is None:
            yield
        s_cur = s_pairs[p][0]
        outs = []
        n_live = -(-t_real // CHUNK)
        for c in range(n_live):
            rs = slice(c * CHUNK, (c + 1) * CHUNK)
            ks_qs = _pdot(jnp.concatenate([k_in[rs], q_dec[rs]], axis=0), s_cur)
            yield
            rhs = bcol[rs] * (vp[rs] - ks_qs[:CHUNK])
            parts = [zeros_half] * (R // CHUNK)
            parts[c] = rhs
            u = _pdot3(t_inv[rs, :], jnp.concatenate(parts, axis=0))
            yield
            parts[c] = u
            u_full = jnp.concatenate(parts, axis=0)
            outs.append(ks_qs[CHUNK:] + _pdot(p_mat[rs, :], u_full))
            parts[c] = k_dec[rs]
            kd = jnp.concatenate(parts, axis=0)
            kd = jnp.concatenate([kd[:, :LANES], kd[:, LANES:]], axis=0)
            last = (c + 1) * CHUNK - 1
            bd = jnp.exp(jnp.concatenate([jnp.broadcast_to(gc[last:last + 1, h0:h0 + 1], (1, LANES)),
                                          jnp.broadcast_to(gc[last:last + 1, h0 + 1:h0 + 2], (1, LANES))],
                                         axis=1))
            s_cur = bd * s_cur + _dot_tn(kd, _blockdiag(u_full.astype(BF16)))
            yield
        o = jnp.concatenate(outs, axis=0)
        zg = _silu(z_all[:n_live * CHUNK, ps])
        s_next[p][0] = s_cur
        return jnp.concatenate([_rms(o[:, :LANES], gn), _rms(o[:, LANES:], gn)], axis=1) * zg

    return [chain(p) for p in range(n_heads // 2)]


def _lockstep(gens):
    done = [None] * len(gens)
    live = list(range(len(gens)))
    while live:
        for i in list(live):
            try:
                next(gens[i])
            except StopIteration as stop:
                done[i] = stop.value
                live.remove(i)
    return done


def _gdn_body(qkv_ref, z_ref, ab_ref, cbuf_ref, s0_ref, cw_ref, alog_ref, dtb_ref, gn_ref,
              y_ref, snew_ref, xprev_sc, s_sc, *, t_real, n_heads, n_double, pre_act):
    t = pl.program_id(1)
    nt = pl.num_programs(1)
    n_streams = qkv_ref.shape[0]
    n_pairs = n_heads // 2

    @pl.when(t == 0)
    def _():
        xprev_sc[...] = cbuf_ref[...]
        for s in range(n_streams):
            for p in range(n_pairs):
                s_sc[s, p] = jnp.concatenate([s0_ref[s, 2 * p], s0_ref[s, 2 * p + 1]], axis=1)

    def pad_rows(x):
        if t_real == ROWS:
            return x
        return jnp.concatenate([x, jnp.zeros((ROWS - t_real, x.shape[1]), x.dtype)], axis=0)

    n_tiles = qkv_ref.shape[1] // t_real
    chains, last = [], []
    for s in range(n_streams):
        state = [[s_sc[s, p]] for p in range(n_pairs)]
        for k in range(n_tiles):
            rs = slice(k * t_real, (k + 1) * t_real)
            x = pad_rows(qkv_ref[s, rs, :])
            nxt = [[None] for _ in range(n_pairs)]
            chains += _gdn_stream(x, xprev_sc[s], pad_rows(ab_ref[s, rs, :]), pad_rows(z_ref[s, rs, :]),
                                  state, nxt, cw_ref[...], alog_ref[...], dtb_ref[...], gn_ref[...],
                                  t_real=t_real, n_heads=n_heads, n_double=n_double, pre_act=pre_act)
            state = nxt
            if not pre_act:
                xprev_sc[s] = x[ROWS - 8:, :]
        last.append(state)
    done = iter(_lockstep(chains))
    for s in range(n_streams):
        for k in range(n_tiles):
            for p in range(n_pairs):
                y_ref[s, k * t_real:(k + 1) * t_real, 2 * p * LANES:2 * (p + 1) * LANES] = (
                    next(done)[:t_real].astype(y_ref.dtype))
        for p in range(n_pairs):
            s_sc[s, p] = last[s][p][0]

    @pl.when(t == nt - 1)
    def _():
        for s in range(n_streams):
            for p in range(n_pairs):
                sp = s_sc[s, p]
                snew_ref[s, 2 * p] = sp[:, :LANES]
                snew_ref[s, 2 * p + 1] = sp[:, LANES:]


def _gdn(aqkv, az, ab, conv_buf8, s0_arr, s0_index, lw, batch, seq, pre_act):
    n_heads = lw['n_heads']
    ns = GDN_STREAMS
    assert batch % ns == 0 and n_heads % 2 == 0
    t_real = min(ROWS, seq)
    step_rows = t_real * (GDN_TILES if seq % (ROWS * GDN_TILES) == 0 else 1)
    nt = seq // step_rows
    c = aqkv.shape[1]
    vw = az.shape[1]
    n_double = max(int(math.ceil(math.log2(min(CHUNK, seq)))) - 1, 0)
    body = functools.partial(_gdn_body, t_real=t_real, n_heads=n_heads, n_double=n_double, pre_act=pre_act)
    tok = lambda w: pl.BlockSpec((ns, step_rows, w), lambda b, t: (b, t, 0))
    s_shape = (ns, n_heads, LANES, LANES)
    return pl.pallas_call(
        body,
        grid=(batch // ns, nt),
        in_specs=[tok(c), tok(vw), tok(LANES),
                  pl.BlockSpec((ns, 8, c), lambda b, t: (b, 0, 0)),
                  pl.BlockSpec((None,) * (s0_arr.ndim - 4) + s_shape, s0_index),
                  _resident(lw['conv_w'].shape), _resident((1, LANES)), _resident((1, LANES)),
                  _resident((1, LANES))],
        out_specs=[tok(vw), pl.BlockSpec(s_shape, lambda b, t: (b, 0, 0, 0))],
        out_shape=[jax.ShapeDtypeStruct((batch, seq, vw), BF16),
                   jax.ShapeDtypeStruct((batch,) + s_shape[1:], F32)],
        scratch_shapes=[pltpu.VMEM((ns, 8, c), F32), pltpu.VMEM((ns, n_heads // 2, LANES, 2 * LANES), F32)],
        compiler_params=_cparams(("parallel", "arbitrary")),
        name="gdn",
    )(aqkv.reshape(batch, seq, c), az.reshape(batch, seq, vw), ab.reshape(batch, seq, LANES),
      conv_buf8, s0_arr, lw['conv_w'], lw['a_log'], lw['dt_bias'], lw['gdn_norm'])


def _sb_block_stages(q, kb, vb, carry, tri, mask):
    z = _dot_nt(q, kb) * (LANES ** -0.5)
    yield
    sp = _softplus(z)
    lf = -sp if mask is None else jnp.where(mask, -sp, 0.0)
    hi, lo = _split(lf)
    after = (jnp.dot(hi, tri, preferred_element_type=F32)
             + jnp.dot(lo, tri, preferred_element_type=F32))
    yield
    a = jnp.exp((z - sp) + after)
    if mask is not None:
        a = jnp.where(mask, a, 0.0)
    out = jnp.exp(carry) * _dot(a, vb)
    return out, jnp.sum(lf, axis=-1, keepdims=True)


def _sb_block(q, kb, vb, carry, tri, mask):
    return _lockstep([_sb_block_stages(q, kb, vb, carry, tri, mask)])[0]


def _sb_prompt_body(q_ref, k_ref, v_ref, o_ref, acc_sc, car_sc, *, n_heads):
    i = pl.program_id(1)
    tq = q_ref.shape[0]
    row = lax.broadcasted_iota(jnp.int32, (tq, tq), 0)
    col = lax.broadcasted_iota(jnp.int32, (tq, tq), 1)
    tri = (row > col).astype(BF16)
    heads = [slice(h * LANES, (h + 1) * LANES) for h in range(n_heads)]
    qs = [q_ref[:, sl] for sl in heads]

    def blocks(off, carries, mask):
        kb = k_ref[pl.ds(off, tq), :]
        vb = v_ref[pl.ds(off, tq), :]
        return _lockstep([_sb_block_stages(qs[h], kb[:, sl], vb[:, sl], carries[h], tri, mask)
                          for h, sl in enumerate(heads)])

    zero = jnp.zeros((tq, 1), F32)
    res = blocks(pl.multiple_of(i * tq, tq), [zero] * n_heads, col < row)
    live = None
    for h, (out, csum) in enumerate(res):
        acc_sc[h] = out
        car_sc[h] = csum
        m = jnp.max(csum)
        live = m if live is None else jnp.maximum(live, m)

    def cond(st):
        j, live = st
        return (j >= 0) & (live > SB_DEAD_LOG)

    def body(st):
        j, _ = st
        carries = [car_sc[h] for h in range(n_heads)]
        res = blocks(pl.multiple_of(j * tq, tq), carries, None)
        live = None
        for h, (out, csum) in enumerate(res):
            acc_sc[h] += out
            carry = carries[h] + csum
            car_sc[h] = carry
            m = jnp.max(carry)
            live = m if live is None else jnp.maximum(live, m)
        return j - 1, live

    lax.while_loop(cond, body, (i - 1, live))
    for h, sl in enumerate(heads):
        o_ref[:, sl] = acc_sc[h].astype(o_ref.dtype)


def _sb_prompt(q, k, v, batch, seq, n_heads):
    tq = min(SB_PROMPT_BLOCK, seq)
    w = n_heads * LANES
    blk = pl.BlockSpec((None, tq, w), lambda b, i: (b, i, 0))
    full = pl.BlockSpec((None, seq, w), lambda b, i: (b, 0, 0), pipeline_mode=pl.Buffered(1))
    return pl.pallas_call(
        functools.partial(_sb_prompt_body, n_heads=n_heads),
        grid=(batch, seq // tq),
        in_specs=[blk, full, full],
        out_specs=blk,
        out_shape=jax.ShapeDtypeStruct((batch, seq, w), BF16),
        scratch_shapes=[pltpu.VMEM((n_heads, tq, LANES), F32), pltpu.VMEM((n_heads, tq, 1), F32)],
        compiler_params=_cparams(("parallel", "arbitrary")),
        name="sb_prompt",
    )(q.reshape(batch, seq, w), k.reshape(batch, seq, w), v.reshape(batch, seq, w))


def _sb_sample_body(q_ref, k_ref, v_ref, kp_hbm, vp_hbm, o_ref, kbuf, vbuf, sem, acc_sc, car_sc,
                    *, n_heads, tk, layer):
    b = pl.program_id(0)
    n_streams = pl.num_programs(0)
    tq = q_ref.shape[0]
    blk_rows = tk * n_heads
    n_blk = kp_hbm.shape[2] // blk_rows
    newest = n_blk - 1

    def copies(stream, j, slot):
        src = pl.ds(j * blk_rows, blk_rows)
        return (pltpu.make_async_copy(kp_hbm.at[layer, stream, src, :], kbuf.at[slot], sem.at[0, slot]),
                pltpu.make_async_copy(vp_hbm.at[layer, stream, src, :], vbuf.at[slot], sem.at[1, slot]))

    def start(stream, j, slot):
        for c in copies(stream, j, slot):
            c.start()

    def wait(stream, j, slot):
        for c in copies(stream, j, slot):
            c.wait()

    def slot_of(j):
        return jnp.where(j == newest, 2 + lax.rem(b, 2), lax.rem(newest - 1 - j, 2))

    @pl.when(b == 0)
    def _():
        start(b, newest, 2)

    @pl.when(b + 1 < n_streams)
    def _():
        start(b + 1, newest, 2 + lax.rem(b + 1, 2))

    if n_blk > 1:
        start(b, newest - 1, 0)

    row = lax.broadcasted_iota(jnp.int32, (tq, LANES), 0)
    col = lax.broadcasted_iota(jnp.int32, (tq, LANES), 1)
    r2 = lax.broadcasted_iota(jnp.int32, (LANES, LANES), 0)
    c2 = lax.broadcasted_iota(jnp.int32, (LANES, LANES), 1)
    tri_new = (r2 > c2).astype(BF16)
    r3 = lax.broadcasted_iota(jnp.int32, (tk, tk), 0)
    c3 = lax.broadcasted_iota(jnp.int32, (tk, tk), 1)
    tri_past = (r3 > c3).astype(BF16)
    pad = jnp.zeros((LANES - tq, LANES), BF16)
    heads = [slice(h * LANES, (h + 1) * LANES) for h in range(n_heads)]
    qs = [q_ref[:, sl] for sl in heads]

    def fold(res, carries, first):
        live = None
        for h, (out, csum) in enumerate(res):
            if first:
                acc_sc[h] = out
                carry = csum
            else:
                acc_sc[h] += out
                carry = carries[h] + csum
            car_sc[h] = carry
            m = jnp.max(carry)
            live = m if live is None else jnp.maximum(live, m)
        return live

    zero = jnp.zeros((tq, 1), F32)
    live = fold(_lockstep([_sb_block_stages(qs[h], jnp.concatenate([k_ref[:, sl], pad], axis=0),
                                            jnp.concatenate([v_ref[:, sl], pad], axis=0),
                                            zero, tri_new, col < row)
                           for h, sl in enumerate(heads)]), None, True)

    def cond(st):
        j, live = st
        return (j >= 0) & (live > SB_DEAD_LOG)

    def body(st):
        j, _ = st
        slot = slot_of(j)
        wait(b, j, slot)

        @pl.when((j > 0) & (j < newest))
        def _():
            start(b, j - 1, slot_of(j - 1))

        carries = [car_sc[h] for h in range(n_heads)]
        res = _lockstep([_sb_block_stages(qs[h], kbuf[slot, pl.ds(h, tk, stride=n_heads), :],
                                          vbuf[slot, pl.ds(h, tk, stride=n_heads), :],
                                          carries[h], tri_past, None) for h in range(n_heads)])
        return j - 1, fold(res, carries, False)

    j_end, _ = lax.while_loop(cond, body, (newest, live))

    @pl.when(j_end == newest)
    def _():
        wait(b, newest, slot_of(newest))

    if n_blk > 1:
        @pl.when(j_end >= 0)
        def _():
            ahead = jnp.minimum(j_end, newest - 1)
            wait(b, ahead, slot_of(ahead))

    for h, sl in enumerate(heads):
        o_ref[:, sl] = acc_sc[h].astype(o_ref.dtype)


def _sb_sample(q, k, v, cache_k, cache_v, layer, batch, seq, n_heads):
    w = n_heads * LANES
    depth, _, past = cache_k.shape[:3]
    tk = min(SB_BLOCK, past)
    assert past % tk == 0
    new = pl.BlockSpec((None, seq, w), lambda b: (b, 0, 0))
    hbm = pl.BlockSpec(memory_space=pl.ANY)
    rows = lambda c: c.reshape(depth, batch, past * n_heads, LANES)
    return pl.pallas_call(
        functools.partial(_sb_sample_body, n_heads=n_heads, tk=tk, layer=layer),
        grid=(batch,),
        in_specs=[new, new, new, hbm, hbm],
        out_specs=new,
        out_shape=jax.ShapeDtypeStruct((batch, seq, w), BF16),
        scratch_shapes=[pltpu.VMEM((4, tk * n_heads, LANES), F32), pltpu.VMEM((4, tk * n_heads, LANES), F32),
                        pltpu.SemaphoreType.DMA((2, 4)),
                        pltpu.VMEM((n_heads, seq, LANES), F32), pltpu.VMEM((n_heads, seq, 1), F32)],
        compiler_params=_cparams(("arbitrary",)),
        name="sb_sample",
    )(q.reshape(batch, seq, w), k.reshape(batch, seq, w), v.reshape(batch, seq, w), rows(cache_k), rows(cache_v))


def _post_body(h_ref, gate_ref, ya_ref, yb_ref, yc_ref, p_ref,
               wpa_ref, wpb_ref, wpc_ref, wo_ref, nf_ref, wf1_ref, wf3_ref, wf2_ref,
               npl_ref, wpg_ref, wpp_ref, nfin_ref, o_ref, *, final, ffn_bounds):
    d = functools.partial(jnp.dot, preferred_element_type=F32)
    h = h_ref[...]
    dm = h.shape[1]
    gates = jax.nn.sigmoid(gate_ref[...])
    merged = (gates[:, :dm] * d(ya_ref[...], wpa_ref[...])
              + gates[:, dm:2 * dm] * d(yb_ref[...], wpb_ref[...])
              + gates[:, 2 * dm:] * d(yc_ref[...], wpc_ref[...]))
    h = h + d(merged.astype(BF16), wo_ref[...])
    xn = _rms(h, nf_ref[...]).astype(BF16)
    ff = None
    for lo, hi in zip(ffn_bounds[:-1], ffn_bounds[1:]):
        cs = slice(lo, hi)
        a1 = d(xn, wf1_ref[:, cs])
        a3 = d(xn, wf3_ref[:, cs])
        part = d((_silu(a1) * a3).astype(BF16), wf2_ref[cs, :])
        ff = part if ff is None else ff + part
    h = h + ff
    xn = _rms(h, npl_ref[...]).astype(BF16)
    h = h + jax.nn.sigmoid(d(xn, wpg_ref[...])) * d(p_ref[...].astype(BF16), wpp_ref[...])
    if final:
        h = _rms(h, nfin_ref[...])
    o_ref[...] = h


def _post(h, gate, ya, yb, yc, p_arr, layer, lw, wts, norm_final, final, tm):
    ntok, dm = h.shape
    tm = min(tm, ntok)
    bw = ya.shape[1]
    pd = p_arr.shape[-1]
    p3 = p_arr.reshape(p_arr.shape[0], ntok, pd)
    tok = lambda w: pl.BlockSpec((tm, w), lambda i: (i, 0))
    hid = wts['w_ffn_out'].shape[1]
    whole = lambda name: (_layer_cols(wts[name], layer), wts[name])
    vec = lambda x: (_resident(x.shape), x)
    weights = [whole('w_pa'), whole('w_pb'), whole('w_pc'), whole('w_o'), vec(lw['norm_ffn']),
               (_layer_cols(wts['w_ffn_in'], layer, hid, 0), wts['w_ffn_in']),
               (_layer_cols(wts['w_ffn_in'], layer, hid, hid), wts['w_ffn_in']),
               whole('w_ffn_out'), vec(lw['norm_ple']), whole('w_ple_gate'), whole('w_ple_proj'),
               vec(norm_final)]
    mid = -(-hid // (2 * MXU_WIDTH)) * MXU_WIDTH if hid % MXU_WIDTH == 0 else hid // 2
    return pl.pallas_call(
        functools.partial(_post_body, final=final, ffn_bounds=(0, mid, hid)),
        grid=(ntok // tm,),
        in_specs=[tok(dm), tok(gate.shape[1]), tok(bw), tok(bw), tok(bw),
                  pl.BlockSpec((None, tm, pd), lambda i: (layer, i, 0))]
                 + [w[0] for w in weights],
        out_specs=tok(dm),
        out_shape=jax.ShapeDtypeStruct((ntok, dm), F32),
        compiler_params=_cparams(("parallel",)),
        name="post",
    )(h, gate, ya, yb, yc, p3, *[w[1] for w in weights])


def _pack_weights(n_heads, n_a, w_in, w_pa, w_pb, w_pc, w_o, w_ffn_in, w_ffn_out, w_ple_gate, w_ple_proj):
    n_ab = 2 * n_heads
    cast = lambda w: w.astype(BF16)
    w16 = cast(w_in)
    w_in_p = jnp.concatenate([w16[:, :, :n_a], w16[:, :, n_a + n_ab:],
                              jnp.pad(w16[:, :, n_a:n_a + n_ab], ((0, 0), (0, 0), (0, LANES - n_ab)))], axis=2)
    return dict(w_in=w_in_p, w_pa=cast(w_pa), w_pb=cast(w_pb), w_pc=cast(w_pc), w_o=cast(w_o),
                w_ffn_in=cast(w_ffn_in), w_ffn_out=cast(w_ffn_out), w_ple_gate=cast(w_ple_gate),
                w_ple_proj=cast(w_ple_proj))


def _layer_weights(i, a_log, dt_bias, w_in, conv_w, norm_mix, gdn_norm, ln_v_g, ln_v_b, w_s, b_s, w_pc,
                   norm_ffn, norm_ple, dec_seq):
    n_heads = a_log.shape[1]
    c = conv_w.shape[2]
    vw = (c - 2 * n_heads * LANES)
    n_uv = 2 * w_s.shape[1] * LANES
    n_sb = w_pc.shape[1]
    n_gate = w_in.shape[2] - (c + vw) - 2 * n_heads - n_uv - 3 * n_sb
    row = lambda x: x.reshape(1, -1).astype(F32)
    lane_pad = lambda x: jnp.pad(x.reshape(1, -1), ((0, 0), (0, LANES - x.size))).astype(F32)
    span = w_s.shape[2]
    rep = ROWS // dec_seq
    lw = dict(
        n_heads=n_heads, n_qkv=c, n_z=vw, n_uv=n_uv, n_sb=n_sb, n_gate=n_gate,
        norm_mix=row(norm_mix[i]),
        conv_w=conv_w[i].astype(F32), a_log=lane_pad(a_log[i]), dt_bias=lane_pad(dt_bias[i]),
        gdn_norm=row(gdn_norm[i]),
        ln_v_g=row(ln_v_g[i]), ln_v_b=row(ln_v_b[i]),
        ws_p=w_s[i].astype(F32),
        bs_p=jnp.repeat(b_s[i].T, LANES, axis=1).astype(F32),
        ws_s=jnp.tile(w_s[i][:, :dec_seq, :dec_seq], (1, rep, rep)).astype(F32),
        bs_s=jnp.tile(jnp.repeat(b_s[i][:, :dec_seq].T, LANES, axis=1), (rep, 1)).astype(F32),
        norm_ffn=row(norm_ffn[i]), norm_ple=row(norm_ple[i]),
    )
    assert span == ROWS
    return lw


def _layer(h, p_arr, layer, conv_buf8, s0_arr, s0_index, caches, lw, wts, norm_final, final, batch, seq, tm):
    n_heads = lw['n_heads']
    sample = caches is not None
    aqkv, az, ab, yb, qb, kb, vb, k4, v4, gate, extra = _in_proj(
        h, lw['norm_mix'], lw, wts, layer, tm, batch, seq, conv_buf8)
    pre_act = extra.ndim == 3
    v_rows = None if pre_act else extra
    ya, s_new = _gdn(aqkv, az, ab, conv_buf8, s0_arr, s0_index, lw, batch, seq, pre_act)
    conv_new = extra[:, 5:, :] if pre_act else aqkv.reshape(batch, seq, -1)[:, seq - 3:, :]
    if sample:
        yc = _sb_sample(qb, kb, vb, caches[0], caches[1], layer, batch, seq, n_heads)
    else:
        yc = _sb_prompt(qb, kb, vb, batch, seq, n_heads)
    bw = ya.shape[-1]
    h = _post(h, gate, ya.reshape(-1, bw), yb, yc.reshape(-1, bw), p_arr, layer, lw, wts, norm_final, final,
              tm)
    kv_shape = (batch, seq, n_heads, LANES)
    return h, conv_new, s_new, k4.reshape(kv_shape), v4.reshape(kv_shape), v_rows


def kernel(x_prompt, x_sample, state_gdn_conv, state_gdn_s, cache_sb_k, cache_sb_v, p_prompt, p_sample, norm_mix, w_in, conv_w, a_log, dt_bias, gdn_norm, w_pa, ln_v_g, ln_v_b, w_s, b_s, w_pb, w_pc, w_o, norm_ffn, w_ffn_in, w_ffn_out, norm_ple, w_ple_gate, w_ple_proj, norm_final):
    depth = w_in.shape[0]
    bp, sp, dm = x_prompt.shape
    bs, ss, _ = x_sample.shape
    n_heads = a_log.shape[1]
    hp = x_prompt.reshape(bp * sp, dm)
    hs = x_sample.reshape(bs * ss, dm)
    nfin = norm_final.reshape(1, dm).astype(F32)
    c = conv_w.shape[2]
    zero_buf = jnp.zeros((bp, 8, c), F32)
    zero_s = jnp.zeros((bp, n_heads, LANES, LANES), F32)
    outs = [[] for _ in range(9)]
    wts = _pack_weights(n_heads, c + (c - 2 * n_heads * LANES), w_in, w_pa, w_pb, w_pc, w_o, w_ffn_in,
                        w_ffn_out, w_ple_gate, w_ple_proj)
    for i in range(depth):
        lw = _layer_weights(i, a_log, dt_bias, w_in, conv_w, norm_mix, gdn_norm, ln_v_g, ln_v_b, w_s, b_s, w_pc,
                            norm_ffn, norm_ple, ss)
        final = i == depth - 1
        hp, pc, ps, pk, pv, _ = _layer(hp, p_prompt, i, zero_buf, zero_s, lambda b, t: (b, 0, 0, 0), None,
                                       lw, wts, nfin, final, bp, sp, 256)
        buf8 = jnp.pad(state_gdn_conv[i], ((0, 0), (8 - state_gdn_conv.shape[2], 0), (0, 0)))
        hs, sc, sn, sk, sv, sm = _layer(hs, p_sample, i, buf8, state_gdn_s,
                                        lambda b, t, i=i: (i, b, 0, 0, 0), (cache_sb_k, cache_sb_v),
                                        lw, wts, nfin, final, bs, ss, 256)
        for lst, val in zip(outs, (pc, ps, pk, pv, sc, sn, sk, sv, sm.reshape(bs, ss, -1))):
            lst.append(val)
    stacked = [jnp.stack(l) for l in outs]
    return (hp.reshape(bp, sp, dm), hs.reshape(bs, ss, dm), *stacked)
```

```python
import functools
import math

import jax
import jax.numpy as jnp
from jax import lax
from jax.experimental import pallas as pl
from jax.experimental.pallas import tpu as pltpu

F32 = jnp.float32
BF16 = jnp.bfloat16

EPS = 1e-6
LOG2E = 1.4426950408889634
CHUNK = 64
LANES = 128
MXU_WIDTH = 256
ROWS = 128
GDN_TILES = 2
GDN_STREAMS = 2
SB_BLOCK = 256
SB_PROMPT_BLOCK = 256
SB_PROMPT_SEQS = 2
SB_DEAD_LOG = -104.0
VMEM_LIMIT = 56 * 1024 * 1024


def _cparams(sem):
    return pltpu.CompilerParams(dimension_semantics=sem, vmem_limit_bytes=VMEM_LIMIT)


def _resident(shape):
    nd = len(shape)
    return pl.BlockSpec(shape, lambda *_: (0,) * nd, pipeline_mode=pl.Buffered(1))


def _layer_cols(arr, layer, width=None, start=0):
    width = arr.shape[2] if width is None else width
    assert start % width == 0
    return pl.BlockSpec((None, arr.shape[1], width), lambda *_: (layer, 0, start // width),
                        pipeline_mode=pl.Buffered(1))


def _rms(x, g):
    return x * lax.rsqrt(jnp.mean(x * x, axis=-1, keepdims=True) + EPS) * g


def _dot(a, b):
    return jnp.dot(a.astype(BF16), b.astype(BF16), preferred_element_type=F32)


def _dot_nt(a, b):
    return lax.dot_general(a.astype(BF16), b.astype(BF16), (((1,), (1,)), ((), ())),
                           preferred_element_type=F32)


def _dot_tn(a, b):
    return lax.dot_general(a.astype(BF16), b.astype(BF16), (((0,), (0,)), ((), ())),
                           preferred_element_type=F32)


def _split(x):
    hi = x.astype(BF16)
    lo = (x - hi.astype(F32)).astype(BF16)
    return hi, lo


def _blockdiag(x):
    z = jnp.zeros((x.shape[0], LANES), x.dtype)
    return jnp.concatenate([jnp.concatenate([x[:, :LANES], z], axis=1),
                            jnp.concatenate([z, x[:, LANES:]], axis=1)], axis=0)


def _pdot(a, b):
    return jnp.dot(a.astype(BF16), _blockdiag(b.astype(BF16)), preferred_element_type=F32)


def _pdot3(a, b):
    ah, al = _split(a)
    bh, bl = _split(b)
    bdh, bdl = _blockdiag(bh), _blockdiag(bl)
    d = functools.partial(jnp.dot, preferred_element_type=F32)
    return d(ah, bdh) + (d(al, bdh) + d(ah, bdl))


def _softplus(x):
    return jnp.maximum(x, 0.0) + jnp.log(1.0 + jnp.exp2(jnp.abs(x) * -LOG2E))


def _silu(x):
    return x * jax.nn.sigmoid(x)


def _gmlp_tile(uv, ln_g, ln_b, ws_ref, bs, span):
    half = uv.shape[1] // 2
    ge = 0.5 * uv * (1.0 + lax.erf(uv * (2.0 ** -0.5)))
    u = ge[:, :half]
    v = ge[:, half:]
    mu = jnp.mean(v, axis=-1, keepdims=True)
    vc = v - mu
    vn = vc * lax.rsqrt(jnp.mean(vc * vc, axis=-1, keepdims=True) + EPS) * ln_g + ln_b
    row = lax.broadcasted_iota(jnp.int32, (ROWS, ROWS), 0)
    col = lax.broadcasted_iota(jnp.int32, (ROWS, ROWS), 1)
    mask = ((row & -span) == (col & -span)) & ((row & (span - 1) & -CHUNK) >= (col & (span - 1) & -CHUNK))
    ys = []
    for g in range(half // LANES):
        sl = slice(g * LANES, (g + 1) * LANES)
        w = jnp.where(mask, ws_ref[g], 0.0)
        ys.append(u[:, sl] * (_dot(w, vn[:, sl]) + bs[:, sl]))
    return jnp.concatenate(ys, axis=1), vn


def _in_proj_body(h_ref, g_ref, wa_ref, wab_ref, wb_ref, wc_ref, wg0_ref, wg1_ref, lng_ref, lnb_ref, ws_ref, bs_ref,
                  cbuf_ref, cw_ref,
                  aqkv_ref, az_ref, ab_ref, yb_ref, qb_ref, kb_ref, vb_ref, k4_ref, v4_ref, gate_ref,
                  extra_ref, xprev_sc, *, span, tiles_per_seq):
    xn = _rms(h_ref[...], g_ref[...]).astype(BF16)
    d = functools.partial(jnp.dot, preferred_element_type=F32)
    tm = h_ref.shape[0]
    n_qkv = aqkv_ref.shape[1]
    i = pl.program_id(0)
    if tiles_per_seq:
        @pl.when(i % tiles_per_seq == 0)
        def _():
            xprev_sc[...] = cbuf_ref[...]

    def gdn_stream():
        gdn_heads = (n_qkv - az_ref.shape[1]) // (2 * LANES)
        for kind, sl in _qkv_groups(gdn_heads, n_qkv):
            x = d(xn, wa_ref[:, sl])
            if tiles_per_seq:
                aqkv_ref[:, sl] = _conv_act(x, xprev_sc[:, sl], cw_ref[:, sl], kind)
                xprev_sc[:, sl] = x[tm - 8:, :]
                extra_ref[:, sl] = x[tm - 8:, :]
            else:
                aqkv_ref[:, sl] = x
            yield
        az_ref[...] = d(xn, wa_ref[:, n_qkv:])
        ab_ref[...] = d(xn, wab_ref[...])

    def gmlp_stream():
        buv = d(xn, wb_ref[...])
        yield
        for t in range(tm // ROWS):
            rs = slice(t * ROWS, (t + 1) * ROWS)
            y, vn = _gmlp_tile(buv[rs], lng_ref[...], lnb_ref[...], ws_ref, bs_ref[...], span)
            yb_ref[rs, :] = y.astype(yb_ref.dtype)
            if not tiles_per_seq:
                extra_ref[rs, :] = vn
            yield

    def sb_gate_stream():
        w = qb_ref.shape[1]
        n_heads = w // LANES
        for j, (b_ref, f_ref) in enumerate(((qb_ref, None), (kb_ref, k4_ref), (vb_ref, v4_ref))):
            r = d(xn, wc_ref[:, j * w:(j + 1) * w])
            b_ref[...] = r.astype(BF16)
            if f_ref is not None:
                for h in range(n_heads):
                    f_ref[pl.ds(h, tm, stride=n_heads), :] = r[:, h * LANES:(h + 1) * LANES]
            yield
        step = 4 * LANES
        for part, wg_ref in enumerate((wg0_ref, wg1_ref)):
            gw = wg_ref.shape[1]
            for c0 in range(0, gw, step):
                gate_ref[:, part * gw + c0:part * gw + c0 + step] = d(xn, wg_ref[:, c0:c0 + step])
                yield

    _lockstep([gdn_stream(), gmlp_stream(), sb_gate_stream()])


def _in_proj(h, norm_g, lw, wts, layer, tm, batch, seq, conv_buf8):
    ntok, dm = h.shape
    tm = min(tm, ntok)
    assert tm % ROWS == 0
    tiles_per_seq = seq // tm if seq % tm == 0 else 0
    span = min(seq, ROWS)
    n_heads = lw['n_sb'] // LANES
    half = lw['n_uv'] // 2
    c = lw['n_qkv']
    tok = lambda w, dt=F32: (pl.BlockSpec((tm, w), lambda i: (i, 0)), jax.ShapeDtypeStruct((ntok, w), dt))
    kv4 = (pl.BlockSpec((tm * n_heads, LANES), lambda i: (i, 0)),
           jax.ShapeDtypeStruct((ntok * n_heads, LANES), F32))
    seq_of = (lambda i: (i // tiles_per_seq, 0, 0)) if tiles_per_seq else (lambda i: (0, 0, 0))
    outs = [tok(c), tok(lw['n_z']), tok(LANES), tok(half, BF16),
            tok(lw['n_sb'], BF16), tok(lw['n_sb'], BF16), tok(lw['n_sb'], BF16), kv4, kv4, tok(lw['n_gate'])]
    if tiles_per_seq:
        outs.append((pl.BlockSpec((None, 8, c), seq_of), jax.ShapeDtypeStruct((batch, 8, c), F32)))
    else:
        outs.append(tok(half))
    ws, bs = (lw['ws_p'], lw['bs_p']) if span == ROWS else (lw['ws_s'], lw['bs_s'])
    w_in = wts['w_in']
    n_a, n_b, n_c, n_g = c + lw['n_z'], lw['n_uv'], 3 * lw['n_sb'], lw['n_gate']
    o_b, o_c, o_g, o_ab = n_a, n_a + n_b, n_a + n_b + n_c, n_a + n_b + n_c + n_g
    return pl.pallas_call(
        functools.partial(_in_proj_body, span=span, tiles_per_seq=tiles_per_seq),
        grid=(ntok // tm,),
        in_specs=[tok(dm)[0], _resident((1, dm)), _layer_cols(w_in, layer, n_a, 0),
                  _layer_cols(w_in, layer, LANES, o_ab), _layer_cols(w_in, layer, n_b, o_b),
                  _layer_cols(w_in, layer, n_c, o_c), _layer_cols(w_in, layer, n_g // 2, o_g),
                  _layer_cols(w_in, layer, n_g // 2, o_g + n_g // 2),
                  _resident((1, half)), _resident((1, half)), _resident(ws.shape), _resident(bs.shape),
                  pl.BlockSpec((None, 8, c), seq_of), _resident(lw['conv_w'].shape)],
        out_specs=[o[0] for o in outs],
        out_shape=[o[1] for o in outs],
        scratch_shapes=[pltpu.VMEM((8, c), F32)],
        compiler_params=_cparams(("arbitrary",)),
        name="in_proj",
    )(h, norm_g, w_in, w_in, w_in, w_in, w_in, w_in, lw['ln_v_g'], lw['ln_v_b'], ws, bs,
      conv_buf8, lw['conv_w'])


def _conv_act(x, prev, cw, kind):
    n_tap = cw.shape[0]
    row8 = lax.broadcasted_iota(jnp.int32, (8, x.shape[1]), 0)
    y = x * cw[n_tap - 1:n_tap, :]
    for sh in range(1, n_tap):
        xs = pltpu.roll(x, sh, 0)
        head = jnp.where(row8 < sh, pltpu.roll(prev, sh, 0), xs[:8])
        xs = jnp.concatenate([head, xs[8:]], axis=0)
        y = y + xs * cw[n_tap - 1 - sh:n_tap - sh, :]
    y = _silu(y)
    if kind == 'v':
        return y
    parts = []
    for h in range(x.shape[1] // LANES):
        m = y[:, h * LANES:(h + 1) * LANES]
        m = m * lax.rsqrt(jnp.sum(m * m, axis=-1, keepdims=True) + EPS)
        parts.append(m * (LANES ** -0.5) if kind == 'q' else m)
    return jnp.concatenate(parts, axis=1)


def _qkv_groups(n_heads, c):
    w = n_heads * LANES
    return (('q', slice(0, w)), ('k', slice(w, 2 * w)), ('v', slice(2 * w, c)))


def _qkv_act(x, prev, cw, n_heads):
    return jnp.concatenate([_conv_act(x[:, sl], prev[:, sl], cw[:, sl], kind)
                            for kind, sl in _qkv_groups(n_heads, x.shape[1])], axis=1)


def _gdn_stream(x, prev, ab, z_all, s_pairs, s_next, cw, alog, dtb, gn, *, t_real, n_heads, n_double,
                pre_act):
    R = ROWS
    L2 = 2 * LANES
    qk_w = n_heads * LANES
    row = lax.broadcasted_iota(jnp.int32, (R, LANES), 0)
    valid = row < t_real
    row2 = lax.broadcasted_iota(jnp.int32, (R, L2), 0)
    col2 = lax.broadcasted_iota(jnp.int32, (R, L2), 1) & (LANES - 1)

    y = x if pre_act else _qkv_act(x, prev, cw, n_heads)

    g = -jnp.exp(alog) * _softplus(ab + dtb)
    beta = jax.nn.sigmoid(ab)
    if t_real != R:
        g = jnp.where(valid, g, 0.0)
        beta = jnp.where(valid, beta, 0.0)
    rm = row & (CHUNK - 1)
    gc = g
    sh = 1
    while sh < CHUNK:
        gc = gc + jnp.where(rm >= sh, pltpu.roll(gc, sh, 0), 0.0)
        sh *= 2
    gc_t = gc.T
    e_gc = jnp.exp(gc)
    g_last = jnp.where(row < CHUNK, gc[CHUNK - 1:CHUNK, :], gc[R - 1:R, :])
    e_rest = jnp.exp(g_last - gc)

    same_blk = (row2 & -CHUNK) == (col2 & -CHUNK)
    incl = same_blk & (row2 >= col2)
    strict = same_blk & (row2 > col2)
    eye = (row2 == col2).astype(F32)
    zeros_half = jnp.zeros((CHUNK, L2), F32)

    def cols(m, h0):
        return jnp.concatenate([jnp.broadcast_to(m[:, h0:h0 + 1], (R, LANES)),
                                jnp.broadcast_to(m[:, h0 + 1:h0 + 2], (R, LANES))], axis=1)

    def chain(p):
        h0 = 2 * p
        ps = slice(h0 * LANES, (h0 + 2) * LANES)
        qp = y[:, h0 * LANES:(h0 + 2) * LANES]
        kp = y[:, qk_w + h0 * LANES:qk_w + (h0 + 2) * LANES]
        vp = y[:, 2 * qk_w + h0 * LANES:2 * qk_w + (h0 + 2) * LANES]
        if t_real != R:
            ok = row2 < t_real
            qp, kp, vp = (jnp.where(ok, m, 0.0) for m in (qp, kp, vp))

        gcol = cols(gc, h0)
        grow = jnp.concatenate([gc_t[h0:h0 + 1, :], gc_t[h0 + 1:h0 + 2, :]], axis=1)
        bcol = cols(beta, n_heads + h0)
        ecol = cols(e_gc, h0)
        rcol = cols(e_rest, h0)

        decay = jnp.where(incl, jnp.exp(gcol - grow), 0.0)
        kq = _dot_nt(jnp.concatenate([kp, qp], axis=0), _blockdiag(kp))
        yield
        a_neg = jnp.where(strict, -(bcol * decay * kq[:R]), 0.0)
        p_mat = decay * kq[R:]

        t_inv = eye + a_neg
        pw = _pdot(a_neg, a_neg) if n_double else None
        yield
        for it in range(n_double):
            if it < n_double - 1:
                both = _pdot(jnp.concatenate([pw, t_inv], axis=0), pw)
                pw = both[:R]
                t_inv = t_inv + both[R:]
            else:
                t_inv = t_inv + _pdot(t_inv, pw)
            yield
        resid = (eye - t_inv) + _pdot3(a_neg, t_inv)
        yield
        t_inv = t_inv + _pdot(t_inv, resid)
        yield

        k_in = ecol * kp
        q_dec = ecol * qp
        k_dec = rcol * kp
        while s_pairs[p][0] is None:
            yield
        s_cur = s_pairs[p][0]
        outs = []
        n_live = -(-t_real // CHUNK)
        for c in range(n_live):
            rs = slice(c * CHUNK, (c + 1) * CHUNK)
            ks_qs = _pdot(jnp.concatenate([k_in[rs], q_dec[rs]], axis=0), s_cur)
            yield
            rhs = bcol[rs] * (vp[rs] - ks_qs[:CHUNK])
            parts = [zeros_half] * (R // CHUNK)
            parts[c] = rhs
            u = _pdot3(t_inv[rs, :], jnp.concatenate(parts, axis=0))
            yield
            parts[c] = u
            u_full = jnp.concatenate(parts, axis=0)
            outs.append(ks_qs[CHUNK:] + _pdot(p_mat[rs, :], u_full))
            parts[c] = k_dec[rs]
            kd = jnp.concatenate(parts, axis=0)
            kd = jnp.concatenate([kd[:, :LANES], kd[:, LANES:]], axis=0)
            last = (c + 1) * CHUNK - 1
            bd = jnp.exp(jnp.concatenate([jnp.broadcast_to(gc[last:last + 1, h0:h0 + 1], (1, LANES)),
                                          jnp.broadcast_to(gc[last:last + 1, h0 + 1:h0 + 2], (1, LANES))],
                                         axis=1))
            s_cur = bd * s_cur + _dot_tn(kd, _blockdiag(u_full.astype(BF16)))
            yield
        o = jnp.concatenate(outs, axis=0)
        zg = _silu(z_all[:n_live * CHUNK, ps])
        s_next[p][0] = s_cur
        return jnp.concatenate([_rms(o[:, :LANES], gn), _rms(o[:, LANES:], gn)], axis=1) * zg

    return [chain(p) for p in range(n_heads // 2)]


def _lockstep(gens):
    done = [None] * len(gens)
    live = list(range(len(gens)))
    while live:
        for i in list(live):
            try:
                next(gens[i])
            except StopIteration as stop:
                done[i] = stop.value
                live.remove(i)
    return done


def _gdn_body(qkv_ref, z_ref, ab_ref, cbuf_ref, s0_ref, cw_ref, alog_ref, dtb_ref, gn_ref,
              y_ref, snew_ref, xprev_sc, s_sc, *, t_real, n_heads, n_double, pre_act):
    t = pl.program_id(1)
    nt = pl.num_programs(1)
    n_streams = qkv_ref.shape[0]
    n_pairs = n_heads // 2

    @pl.when(t == 0)
    def _():
        xprev_sc[...] = cbuf_ref[...]
        for s in range(n_streams):
            for p in range(n_pairs):
                s_sc[s, p] = jnp.concatenate([s0_ref[s, 2 * p], s0_ref[s, 2 * p + 1]], axis=1)

    def pad_rows(x):
        if t_real == ROWS:
            return x
        return jnp.concatenate([x, jnp.zeros((ROWS - t_real, x.shape[1]), x.dtype)], axis=0)

    n_tiles = qkv_ref.shape[1] // t_real
    chains, last = [], []
    for s in range(n_streams):
        state = [[s_sc[s, p]] for p in range(n_pairs)]
        for k in range(n_tiles):
            rs = slice(k * t_real, (k + 1) * t_real)
            x = pad_rows(qkv_ref[s, rs, :])
            nxt = [[None] for _ in range(n_pairs)]
            chains += _gdn_stream(x, xprev_sc[s], pad_rows(ab_ref[s, rs, :]), pad_rows(z_ref[s, rs, :]),
                                  state, nxt, cw_ref[...], alog_ref[...], dtb_ref[...], gn_ref[...],
                                  t_real=t_real, n_heads=n_heads, n_double=n_double, pre_act=pre_act)
            state = nxt
            if not pre_act:
                xprev_sc[s] = x[ROWS - 8:, :]
        last.append(state)
    done = iter(_lockstep(chains))
    for s in range(n_streams):
        for k in range(n_tiles):
            for p in range(n_pairs):
                y_ref[s, k * t_real:(k + 1) * t_real, 2 * p * LANES:2 * (p + 1) * LANES] = (
                    next(done)[:t_real].astype(y_ref.dtype))
        for p in range(n_pairs):
            s_sc[s, p] = last[s][p][0]

    @pl.when(t == nt - 1)
    def _():
        for s in range(n_streams):
            for p in range(n_pairs):
                sp = s_sc[s, p]
                snew_ref[s, 2 * p] = sp[:, :LANES]
                snew_ref[s, 2 * p + 1] = sp[:, LANES:]


def _gdn(aqkv, az, ab, conv_buf8, s0_arr, s0_index, lw, batch, seq, pre_act):
    n_heads = lw['n_heads']
    ns = GDN_STREAMS
    assert batch % ns == 0 and n_heads % 2 == 0
    t_real = min(ROWS, seq)
    step_rows = t_real * (GDN_TILES if seq % (ROWS * GDN_TILES) == 0 else 1)
    nt = seq // step_rows
    c = aqkv.shape[1]
    vw = az.shape[1]
    n_double = max(int(math.ceil(math.log2(min(CHUNK, seq)))) - 1, 0)
    body = functools.partial(_gdn_body, t_real=t_real, n_heads=n_heads, n_double=n_double, pre_act=pre_act)
    tok = lambda w: pl.BlockSpec((ns, step_rows, w), lambda b, t: (b, t, 0))
    s_shape = (ns, n_heads, LANES, LANES)
    return pl.pallas_call(
        body,
        grid=(batch // ns, nt),
        in_specs=[tok(c), tok(vw), tok(LANES),
                  pl.BlockSpec((ns, 8, c), lambda b, t: (b, 0, 0)),
                  pl.BlockSpec((None,) * (s0_arr.ndim - 4) + s_shape, s0_index),
                  _resident(lw['conv_w'].shape), _resident((1, LANES)), _resident((1, LANES)),
                  _resident((1, LANES))],
        out_specs=[tok(vw), pl.BlockSpec(s_shape, lambda b, t: (b, 0, 0, 0))],
        out_shape=[jax.ShapeDtypeStruct((batch, seq, vw), BF16),
                   jax.ShapeDtypeStruct((batch,) + s_shape[1:], F32)],
        scratch_shapes=[pltpu.VMEM((ns, 8, c), F32), pltpu.VMEM((ns, n_heads // 2, LANES, 2 * LANES), F32)],
        compiler_params=_cparams(("parallel", "arbitrary")),
        name="gdn",
    )(aqkv.reshape(batch, seq, c), az.reshape(batch, seq, vw), ab.reshape(batch, seq, LANES),
      conv_buf8, s0_arr, lw['conv_w'], lw['a_log'], lw['dt_bias'], lw['gdn_norm'])


def _sb_block_stages(q, kb, vb, carry, tri, mask):
    z = _dot_nt(q, kb) * (LANES ** -0.5)
    yield
    sp = _softplus(z)
    lf = -sp if mask is None else jnp.where(mask, -sp, 0.0)
    hi, lo = _split(lf)
    after = (jnp.dot(hi, tri, preferred_element_type=F32)
             + jnp.dot(lo, tri, preferred_element_type=F32))
    yield
    a = jnp.exp((z - sp) + after)
    if mask is not None:
        a = jnp.where(mask, a, 0.0)
    out = jnp.exp(carry) * _dot(a, vb)
    return out, jnp.sum(lf, axis=-1, keepdims=True)


def _sb_block(q, kb, vb, carry, tri, mask):
    return _lockstep([_sb_block_stages(q, kb, vb, carry, tri, mask)])[0]


def _sb_prompt_body(q_ref, k_ref, v_ref, o_ref, acc_sc, car_sc, *, n_heads):
    i = pl.program_id(1)
    n_seq, tq = q_ref.shape[:2]
    row = lax.broadcasted_iota(jnp.int32, (tq, tq), 0)
    col = lax.broadcasted_iota(jnp.int32, (tq, tq), 1)
    tri = (row > col).astype(BF16)
    chains = [(s, slice(h * LANES, (h + 1) * LANES)) for s in range(n_seq) for h in range(n_heads)]
    qs = [q_ref[s, :, sl] for s, sl in chains]

    def blocks(off, carries, mask):
        kb = [k_ref[s, pl.ds(off, tq), :] for s in range(n_seq)]
        vb = [v_ref[s, pl.ds(off, tq), :] for s in range(n_seq)]
        return _lockstep([_sb_block_stages(qs[c], kb[s][:, sl], vb[s][:, sl], carries[c], tri, mask)
                          for c, (s, sl) in enumerate(chains)])

    def fold(res, carries):
        live = None
        for c, (out, csum) in enumerate(res):
            if carries is None:
                acc_sc[c] = out
                carry = csum
            else:
                acc_sc[c] += out
                carry = carries[c] + csum
            car_sc[c] = carry
            m = jnp.max(carry)
            live = m if live is None else jnp.maximum(live, m)
        return live

    zero = jnp.zeros((tq, 1), F32)
    live = fold(blocks(pl.multiple_of(i * tq, tq), [zero] * len(chains), col < row), None)

    def cond(st):
        j, live = st
        return (j >= 0) & (live > SB_DEAD_LOG)

    def body(st):
        j, _ = st
        carries = [car_sc[c] for c in range(len(chains))]
        return j - 1, fold(blocks(pl.multiple_of(j * tq, tq), carries, None), carries)

    lax.while_loop(cond, body, (i - 1, live))
    for c, (s, sl) in enumerate(chains):
        o_ref[s, :, sl] = acc_sc[c].astype(o_ref.dtype)


def _sb_prompt(q, k, v, batch, seq, n_heads):
    tq = min(SB_PROMPT_BLOCK, seq)
    w = n_heads * LANES
    ns = SB_PROMPT_SEQS if batch % SB_PROMPT_SEQS == 0 else 1
    blk = pl.BlockSpec((ns, tq, w), lambda b, i: (b, i, 0))
    full = pl.BlockSpec((ns, seq, w), lambda b, i: (b, 0, 0), pipeline_mode=pl.Buffered(1))
    return pl.pallas_call(
        functools.partial(_sb_prompt_body, n_heads=n_heads),
        grid=(batch // ns, seq // tq),
        in_specs=[blk, full, full],
        out_specs=blk,
        out_shape=jax.ShapeDtypeStruct((batch, seq, w), BF16),
        scratch_shapes=[pltpu.VMEM((ns * n_heads, tq, LANES), F32), pltpu.VMEM((ns * n_heads, tq, 1), F32)],
        compiler_params=_cparams(("parallel", "arbitrary")),
        name="sb_prompt",
    )(q.reshape(batch, seq, w), k.reshape(batch, seq, w), v.reshape(batch, seq, w))


def _sb_sample_body(q_ref, k_ref, v_ref, kp_hbm, vp_hbm, o_ref, kbuf, vbuf, sem, acc_sc, car_sc,
                    *, n_heads, tk, layer):
    b = pl.program_id(0)
    n_streams = pl.num_programs(0)
    tq = q_ref.shape[0]
    blk_rows = tk * n_heads
    n_blk = kp_hbm.shape[2] // blk_rows
    newest = n_blk - 1

    def copies(stream, j, slot):
        src = pl.ds(j * blk_rows, blk_rows)
        return (pltpu.make_async_copy(kp_hbm.at[layer, stream, src, :], kbuf.at[slot], sem.at[0, slot]),
                pltpu.make_async_copy(vp_hbm.at[layer, stream, src, :], vbuf.at[slot], sem.at[1, slot]))

    def start(stream, j, slot):
        for c in copies(stream, j, slot):
            c.start()

    def wait(stream, j, slot):
        for c in copies(stream, j, slot):
            c.wait()

    def slot_of(j):
        return jnp.where(j == newest, 2 + lax.rem(b, 2), lax.rem(newest - 1 - j, 2))

    @pl.when(b == 0)
    def _():
        start(b, newest, 2)

    @pl.when(b + 1 < n_streams)
    def _():
        start(b + 1, newest, 2 + lax.rem(b + 1, 2))

    if n_blk > 1:
        start(b, newest - 1, 0)

    row = lax.broadcasted_iota(jnp.int32, (tq, LANES), 0)
    col = lax.broadcasted_iota(jnp.int32, (tq, LANES), 1)
    r2 = lax.broadcasted_iota(jnp.int32, (LANES, LANES), 0)
    c2 = lax.broadcasted_iota(jnp.int32, (LANES, LANES), 1)
    tri_new = (r2 > c2).astype(BF16)
    r3 = lax.broadcasted_iota(jnp.int32, (tk, tk), 0)
    c3 = lax.broadcasted_iota(jnp.int32, (tk, tk), 1)
    tri_past = (r3 > c3).astype(BF16)
    pad = jnp.zeros((LANES - tq, LANES), BF16)
    heads = [slice(h * LANES, (h + 1) * LANES) for h in range(n_heads)]
    qs = [q_ref[:, sl] for sl in heads]

    def fold(res, carries, first):
        live = None
        for h, (out, csum) in enumerate(res):
            if first:
                acc_sc[h] = out
                carry = csum
            else:
                acc_sc[h] += out
                carry = carries[h] + csum
            car_sc[h] = carry
            m = jnp.max(carry)
            live = m if live is None else jnp.maximum(live, m)
        return live

    zero = jnp.zeros((tq, 1), F32)
    live = fold(_lockstep([_sb_block_stages(qs[h], jnp.concatenate([k_ref[:, sl], pad], axis=0),
                                            jnp.concatenate([v_ref[:, sl], pad], axis=0),
                                            zero, tri_new, col < row)
                           for h, sl in enumerate(heads)]), None, True)

    def cond(st):
        j, live = st
        return (j >= 0) & (live > SB_DEAD_LOG)

    def body(st):
        j, _ = st
        slot = slot_of(j)
        wait(b, j, slot)

        @pl.when((j > 0) & (j < newest))
        def _():
            start(b, j - 1, slot_of(j - 1))

        carries = [car_sc[h] for h in range(n_heads)]
        res = _lockstep([_sb_block_stages(qs[h], kbuf[slot, pl.ds(h, tk, stride=n_heads), :],
                                          vbuf[slot, pl.ds(h, tk, stride=n_heads), :],
                                          carries[h], tri_past, None) for h in range(n_heads)])
        return j - 1, fold(res, carries, False)

    j_end, _ = lax.while_loop(cond, body, (newest, live))

    @pl.when(j_end == newest)
    def _():
        wait(b, newest, slot_of(newest))

    if n_blk > 1:
        @pl.when(j_end >= 0)
        def _():
            ahead = jnp.minimum(j_end, newest - 1)
            wait(b, ahead, slot_of(ahead))

    for h, sl in enumerate(heads):
        o_ref[:, sl] = acc_sc[h].astype(o_ref.dtype)


def _sb_sample(q, k, v, cache_k, cache_v, layer, batch, seq, n_heads):
    w = n_heads * LANES
    depth, _, past = cache_k.shape[:3]
    tk = min(SB_BLOCK, past)
    assert past % tk == 0
    new = pl.BlockSpec((None, seq, w), lambda b: (b, 0, 0))
    hbm = pl.BlockSpec(memory_space=pl.ANY)
    rows = lambda c: c.reshape(depth, batch, past * n_heads, LANES)
    return pl.pallas_call(
        functools.partial(_sb_sample_body, n_heads=n_heads, tk=tk, layer=layer),
        grid=(batch,),
        in_specs=[new, new, new, hbm, hbm],
        out_specs=new,
        out_shape=jax.ShapeDtypeStruct((batch, seq, w), BF16),
        scratch_shapes=[pltpu.VMEM((4, tk * n_heads, LANES), F32), pltpu.VMEM((4, tk * n_heads, LANES), F32),
                        pltpu.SemaphoreType.DMA((2, 4)),
                        pltpu.VMEM((n_heads, seq, LANES), F32), pltpu.VMEM((n_heads, seq, 1), F32)],
        compiler_params=_cparams(("arbitrary",)),
        name="sb_sample",
    )(q.reshape(batch, seq, w), k.reshape(batch, seq, w), v.reshape(batch, seq, w), rows(cache_k), rows(cache_v))


def _post_body(h_ref, gate_ref, ya_ref, yb_ref, yc_ref, p_ref,
               wpa_ref, wpb_ref, wpc_ref, wo_ref, nf_ref, wf1_ref, wf3_ref, wf2_ref,
               npl_ref, wpg_ref, wpp_ref, nfin_ref, o_ref, *, final, ffn_bounds):
    d = functools.partial(jnp.dot, preferred_element_type=F32)
    h = h_ref[...]
    dm = h.shape[1]
    gates = jax.nn.sigmoid(gate_ref[...])
    merged = (gates[:, :dm] * d(ya_ref[...], wpa_ref[...])
              + gates[:, dm:2 * dm] * d(yb_ref[...], wpb_ref[...])
              + gates[:, 2 * dm:] * d(yc_ref[...], wpc_ref[...]))
    h = h + d(merged.astype(BF16), wo_ref[...])
    xn = _rms(h, nf_ref[...]).astype(BF16)
    ff = None
    for lo, hi in zip(ffn_bounds[:-1], ffn_bounds[1:]):
        cs = slice(lo, hi)
        a1 = d(xn, wf1_ref[:, cs])
        a3 = d(xn, wf3_ref[:, cs])
        part = d((_silu(a1) * a3).astype(BF16), wf2_ref[cs, :])
        ff = part if ff is None else ff + part
    h = h + ff
    xn = _rms(h, npl_ref[...]).astype(BF16)
    h = h + jax.nn.sigmoid(d(xn, wpg_ref[...])) * d(p_ref[...].astype(BF16), wpp_ref[...])
    if final:
        h = _rms(h, nfin_ref[...])
    o_ref[...] = h


def _post(h, gate, ya, yb, yc, p_arr, layer, lw, wts, norm_final, final, tm):
    ntok, dm = h.shape
    tm = min(tm, ntok)
    bw = ya.shape[1]
    pd = p_arr.shape[-1]
    p3 = p_arr.reshape(p_arr.shape[0], ntok, pd)
    tok = lambda w: pl.BlockSpec((tm, w), lambda i: (i, 0))
    hid = wts['w_ffn_out'].shape[1]
    whole = lambda name: (_layer_cols(wts[name], layer), wts[name])
    vec = lambda x: (_resident(x.shape), x)
    weights = [whole('w_pa'), whole('w_pb'), whole('w_pc'), whole('w_o'), vec(lw['norm_ffn']),
               (_layer_cols(wts['w_ffn_in'], layer, hid, 0), wts['w_ffn_in']),
               (_layer_cols(wts['w_ffn_in'], layer, hid, hid), wts['w_ffn_in']),
               whole('w_ffn_out'), vec(lw['norm_ple']), whole('w_ple_gate'), whole('w_ple_proj'),
               vec(norm_final)]
    mid = -(-hid // (2 * MXU_WIDTH)) * MXU_WIDTH if hid % MXU_WIDTH == 0 else hid // 2
    return pl.pallas_call(
        functools.partial(_post_body, final=final, ffn_bounds=(0, mid, hid)),
        grid=(ntok // tm,),
        in_specs=[tok(dm), tok(gate.shape[1]), tok(bw), tok(bw), tok(bw),
                  pl.BlockSpec((None, tm, pd), lambda i: (layer, i, 0))]
                 + [w[0] for w in weights],
        out_specs=tok(dm),
        out_shape=jax.ShapeDtypeStruct((ntok, dm), F32),
        compiler_params=_cparams(("parallel",)),
        name="post",
    )(h, gate, ya, yb, yc, p3, *[w[1] for w in weights])


def _pack_weights(n_heads, n_a, w_in, w_pa, w_pb, w_pc, w_o, w_ffn_in, w_ffn_out, w_ple_gate, w_ple_proj):
    n_ab = 2 * n_heads
    cast = lambda w: w.astype(BF16)
    w16 = cast(w_in)
    w_in_p = jnp.concatenate([w16[:, :, :n_a], w16[:, :, n_a + n_ab:],
                              jnp.pad(w16[:, :, n_a:n_a + n_ab], ((0, 0), (0, 0), (0, LANES - n_ab)))], axis=2)
    return dict(w_in=w_in_p, w_pa=cast(w_pa), w_pb=cast(w_pb), w_pc=cast(w_pc), w_o=cast(w_o),
                w_ffn_in=cast(w_ffn_in), w_ffn_out=cast(w_ffn_out), w_ple_gate=cast(w_ple_gate),
                w_ple_proj=cast(w_ple_proj))


def _layer_weights(i, a_log, dt_bias, w_in, conv_w, norm_mix, gdn_norm, ln_v_g, ln_v_b, w_s, b_s, w_pc,
                   norm_ffn, norm_ple, dec_seq):
    n_heads = a_log.shape[1]
    c = conv_w.shape[2]
    vw = (c - 2 * n_heads * LANES)
    n_uv = 2 * w_s.shape[1] * LANES
    n_sb = w_pc.shape[1]
    n_gate = w_in.shape[2] - (c + vw) - 2 * n_heads - n_uv - 3 * n_sb
    row = lambda x: x.reshape(1, -1).astype(F32)
    lane_pad = lambda x: jnp.pad(x.reshape(1, -1), ((0, 0), (0, LANES - x.size))).astype(F32)
    span = w_s.shape[2]
    rep = ROWS // dec_seq
    lw = dict(
        n_heads=n_heads, n_qkv=c, n_z=vw, n_uv=n_uv, n_sb=n_sb, n_gate=n_gate,
        norm_mix=row(norm_mix[i]),
        conv_w=conv_w[i].astype(F32), a_log=lane_pad(a_log[i]), dt_bias=lane_pad(dt_bias[i]),
        gdn_norm=row(gdn_norm[i]),
        ln_v_g=row(ln_v_g[i]), ln_v_b=row(ln_v_b[i]),
        ws_p=w_s[i].astype(F32),
        bs_p=jnp.repeat(b_s[i].T, LANES, axis=1).astype(F32),
        ws_s=jnp.tile(w_s[i][:, :dec_seq, :dec_seq], (1, rep, rep)).astype(F32),
        bs_s=jnp.tile(jnp.repeat(b_s[i][:, :dec_seq].T, LANES, axis=1), (rep, 1)).astype(F32),
        norm_ffn=row(norm_ffn[i]), norm_ple=row(norm_ple[i]),
    )
    assert span == ROWS
    return lw


def _layer(h, p_arr, layer, conv_buf8, s0_arr, s0_index, caches, lw, wts, norm_final, final, batch, seq, tm):
    n_heads = lw['n_heads']
    sample = caches is not None
    aqkv, az, ab, yb, qb, kb, vb, k4, v4, gate, extra = _in_proj(
        h, lw['norm_mix'], lw, wts, layer, tm, batch, seq, conv_buf8)
    pre_act = extra.ndim == 3
    v_rows = None if pre_act else extra
    ya, s_new = _gdn(aqkv, az, ab, conv_buf8, s0_arr, s0_index, lw, batch, seq, pre_act)
    conv_new = extra[:, 5:, :] if pre_act else aqkv.reshape(batch, seq, -1)[:, seq - 3:, :]
    if sample:
        yc = _sb_sample(qb, kb, vb, caches[0], caches[1], layer, batch, seq, n_heads)
    else:
        yc = _sb_prompt(qb, kb, vb, batch, seq, n_heads)
    bw = ya.shape[-1]
    h = _post(h, gate, ya.reshape(-1, bw), yb, yc.reshape(-1, bw), p_arr, layer, lw, wts, norm_final, final,
              tm)
    kv_shape = (batch, seq, n_heads, LANES)
    return h, conv_new, s_new, k4.reshape(kv_shape), v4.reshape(kv_shape), v_rows


def kernel(x_prompt, x_sample, state_gdn_conv, state_gdn_s, cache_sb_k, cache_sb_v, p_prompt, p_sample, norm_mix, w_in, conv_w, a_log, dt_bias, gdn_norm, w_pa, ln_v_g, ln_v_b, w_s, b_s, w_pb, w_pc, w_o, norm_ffn, w_ffn_in, w_ffn_out, norm_ple, w_ple_gate, w_ple_proj, norm_final):
    depth = w_in.shape[0]
    bp, sp, dm = x_prompt.shape
    bs, ss, _ = x_sample.shape
    n_heads = a_log.shape[1]
    hp = x_prompt.reshape(bp * sp, dm)
    hs = x_sample.reshape(bs * ss, dm)
    nfin = norm_final.reshape(1, dm).astype(F32)
    c = conv_w.shape[2]
    zero_buf = jnp.zeros((bp, 8, c), F32)
    zero_s = jnp.zeros((bp, n_heads, LANES, LANES), F32)
    outs = [[] for _ in range(9)]
    wts = _pack_weights(n_heads, c + (c - 2 * n_heads * LANES), w_in, w_pa, w_pb, w_pc, w_o, w_ffn_in,
                        w_ffn_out, w_ple_gate, w_ple_proj)
    for i in range(depth):
        lw = _layer_weights(i, a_log, dt_bias, w_in, conv_w, norm_mix, gdn_norm, ln_v_g, ln_v_b, w_s, b_s, w_pc,
                            norm_ffn, norm_ple, ss)
        final = i == depth - 1
        hp, pc, ps, pk, pv, _ = _layer(hp, p_prompt, i, zero_buf, zero_s, lambda b, t: (b, 0, 0, 0), None,
                                       lw, wts, nfin, final, bp, sp, 256)
        buf8 = jnp.pad(state_gdn_conv[i], ((0, 0), (8 - state_gdn_conv.shape[2], 0), (0, 0)))
        hs, sc, sn, sk, sv, sm = _layer(hs, p_sample, i, buf8, state_gdn_s,
                                        lambda b, t, i=i: (i, b, 0, 0, 0), (cache_sb_k, cache_sb_v),
                                        lw, wts, nfin, final, bs, ss, 256)
        for lst, val in zip(outs, (pc, ps, pk, pv, sc, sn, sk, sv, sm.reshape(bs, ss, -1))):
            lst.append(val)
    stacked = [jnp.stack(l) for l in outs]
    return (hp.reshape(bp, sp, dm), hs.reshape(bs, ss, dm), *stacked)
```

```python
import functools
import math

import jax
import jax.numpy as jnp
from jax import lax
from jax.experimental import pallas as pl
from jax.experimental.pallas import tpu as pltpu

F32 = jnp.float32
BF16 = jnp.bfloat16

EPS = 1e-6
LOG2E = 1.4426950408889634
CHUNK = 64
LANES = 128
MXU_WIDTH = 256
ROWS = 128
GDN_TILES = 2
GDN_STREAMS = 2
SB_BLOCK = 256
SB_PROMPT_BLOCK = 256
SB_PROMPT_SEQS = 2
SB_DEAD_LOG = -104.0
VMEM_LIMIT = 56 * 1024 * 1024


def _cparams(sem):
    return pltpu.CompilerParams(dimension_semantics=sem, vmem_limit_bytes=VMEM_LIMIT)


def _resident(shape):
    nd = len(shape)
    return pl.BlockSpec(shape, lambda *_: (0,) * nd, pipeline_mode=pl.Buffered(1))


def _layer_cols(arr, layer, width=None, start=0):
    width = arr.shape[2] if width is None else width
    assert start % width == 0
    return pl.BlockSpec((None, arr.shape[1], width), lambda *_: (layer, 0, start // width),
                        pipeline_mode=pl.Buffered(1))


def _rms(x, g):
    return x * lax.rsqrt(jnp.mean(x * x, axis=-1, keepdims=True) + EPS) * g


def _dot(a, b):
    return jnp.dot(a.astype(BF16), b.astype(BF16), preferred_element_type=F32)


def _dot_nt(a, b):
    return lax.dot_general(a.astype(BF16), b.astype(BF16), (((1,), (1,)), ((), ())),
                           preferred_element_type=F32)


def _dot_tn(a, b):
    return lax.dot_general(a.astype(BF16), b.astype(BF16), (((0,), (0,)), ((), ())),
                           preferred_element_type=F32)


def _split(x):
    hi = x.astype(BF16)
    lo = (x - hi.astype(F32)).astype(BF16)
    return hi, lo


def _blockdiag(x):
    z = jnp.zeros((x.shape[0], LANES), x.dtype)
    return jnp.concatenate([jnp.concatenate([x[:, :LANES], z], axis=1),
                            jnp.concatenate([z, x[:, LANES:]], axis=1)], axis=0)


def _pdot(a, b):
    return jnp.dot(a.astype(BF16), _blockdiag(b.astype(BF16)), preferred_element_type=F32)


def _pdot3(a, b):
    ah, al = _split(a)
    bh, bl = _split(b)
    bdh, bdl = _blockdiag(bh), _blockdiag(bl)
    d = functools.partial(jnp.dot, preferred_element_type=F32)
    return d(ah, bdh) + (d(al, bdh) + d(ah, bdl))


def _softplus(x):
    return jnp.maximum(x, 0.0) + jnp.log(1.0 + jnp.exp2(jnp.abs(x) * -LOG2E))


def _silu(x):
    return x * jax.nn.sigmoid(x)


def _gmlp_tile(uv, ln_g, ln_b, ws_ref, bs, span):
    half = uv.shape[1] // 2
    ge = 0.5 * uv * (1.0 + lax.erf(uv * (2.0 ** -0.5)))
    u = ge[:, :half]
    v = ge[:, half:]
    mu = jnp.mean(v, axis=-1, keepdims=True)
    vc = v - mu
    vn = vc * lax.rsqrt(jnp.mean(vc * vc, axis=-1, keepdims=True) + EPS) * ln_g + ln_b
    row = lax.broadcasted_iota(jnp.int32, (ROWS, ROWS), 0)
    col = lax.broadcasted_iota(jnp.int32, (ROWS, ROWS), 1)
    mask = ((row & -span) == (col & -span)) & ((row & (span - 1) & -CHUNK) >= (col & (span - 1) & -CHUNK))
    ys = []
    for g in range(half // LANES):
        sl = slice(g * LANES, (g + 1) * LANES)
        w = jnp.where(mask, ws_ref[g], 0.0)
        ys.append(u[:, sl] * (_dot(w, vn[:, sl]) + bs[:, sl]))
    return jnp.concatenate(ys, axis=1), vn


def _in_proj_body(h_ref, g_ref, wa_ref, wab_ref, wb_ref, wc_ref, wg0_ref, wg1_ref, lng_ref, lnb_ref, ws_ref, bs_ref,
                  cbuf_ref, cw_ref,
                  aqkv_ref, az_ref, ab_ref, yb_ref, qb_ref, kb_ref, vb_ref, k4_ref, v4_ref, gate_ref,
                  extra_ref, xprev_sc, *, span, tiles_per_seq):
    xn = _rms(h_ref[...], g_ref[...]).astype(BF16)
    d = functools.partial(jnp.dot, preferred_element_type=F32)
    tm = h_ref.shape[0]
    n_qkv = aqkv_ref.shape[1]
    i = pl.program_id(0)
    if tiles_per_seq:
        @pl.when(i % tiles_per_seq == 0)
        def _():
            xprev_sc[...] = cbuf_ref[...]

    def gdn_stream():
        gdn_heads = (n_qkv - az_ref.shape[1]) // (2 * LANES)
        for kind, sl in _qkv_groups(gdn_heads, n_qkv):
            x = d(xn, wa_ref[:, sl])
            if tiles_per_seq:
                aqkv_ref[:, sl] = _conv_act(x, xprev_sc[:, sl], cw_ref[:, sl], kind)
                xprev_sc[:, sl] = x[tm - 8:, :]
                extra_ref[:, sl] = x[tm - 8:, :]
            else:
                aqkv_ref[:, sl] = x
            yield
        az_ref[...] = d(xn, wa_ref[:, n_qkv:])
        ab_ref[...] = d(xn, wab_ref[...])

    def gmlp_stream():
        buv = d(xn, wb_ref[...])
        yield
        for t in range(tm // ROWS):
            rs = slice(t * ROWS, (t + 1) * ROWS)
            y, vn = _gmlp_tile(buv[rs], lng_ref[...], lnb_ref[...], ws_ref, bs_ref[...], span)
            yb_ref[rs, :] = y.astype(yb_ref.dtype)
            if not tiles_per_seq:
                extra_ref[rs, :] = vn
            yield

    def sb_gate_stream():
        w = qb_ref.shape[1]
        n_heads = w // LANES
        for j, (b_ref, f_ref) in enumerate(((qb_ref, None), (kb_ref, k4_ref), (vb_ref, v4_ref))):
            r = d(xn, wc_ref[:, j * w:(j + 1) * w])
            b_ref[...] = r.astype(BF16)
            if f_ref is not None:
                for h in range(n_heads):
                    f_ref[pl.ds(h, tm, stride=n_heads), :] = r[:, h * LANES:(h + 1) * LANES]
            yield
        step = 4 * LANES
        for part, wg_ref in enumerate((wg0_ref, wg1_ref)):
            gw = wg_ref.shape[1]
            for c0 in range(0, gw, step):
                gate_ref[:, part * gw + c0:part * gw + c0 + step] = d(xn, wg_ref[:, c0:c0 + step])
                yield

    _lockstep([gdn_stream(), gmlp_stream(), sb_gate_stream()])


def _in_proj(h, norm_g, lw, wts, layer, tm, batch, seq, conv_buf8):
    ntok, dm = h.shape
    tm = min(tm, ntok)
    assert tm % ROWS == 0
    tiles_per_seq = seq // tm if seq % tm == 0 else 0
    span = min(seq, ROWS)
    n_heads = lw['n_sb'] // LANES
    half = lw['n_uv'] // 2
    c = lw['n_qkv']
    tok = lambda w, dt=F32: (pl.BlockSpec((tm, w), lambda i: (i, 0)), jax.ShapeDtypeStruct((ntok, w), dt))
    kv4 = (pl.BlockSpec((tm * n_heads, LANES), lambda i: (i, 0)),
           jax.ShapeDtypeStruct((ntok * n_heads, LANES), F32))
    seq_of = (lambda i: (i // tiles_per_seq, 0, 0)) if tiles_per_seq else (lambda i: (0, 0, 0))
    outs = [tok(c), tok(lw['n_z']), tok(LANES), tok(half, BF16),
            tok(lw['n_sb'], BF16), tok(lw['n_sb'], BF16), tok(lw['n_sb'], BF16), kv4, kv4, tok(lw['n_gate'])]
    if tiles_per_seq:
        outs.append((pl.BlockSpec((None, 8, c), seq_of), jax.ShapeDtypeStruct((batch, 8, c), F32)))
    else:
        outs.append(tok(half))
    ws, bs = (lw['ws_p'], lw['bs_p']) if span == ROWS else (lw['ws_s'], lw['bs_s'])
    w_in = wts['w_in']
    n_a, n_b, n_c, n_g = c + lw['n_z'], lw['n_uv'], 3 * lw['n_sb'], lw['n_gate']
    o_b, o_c, o_g, o_ab = n_a, n_a + n_b, n_a + n_b + n_c, n_a + n_b + n_c + n_g
    return pl.pallas_call(
        functools.partial(_in_proj_body, span=span, tiles_per_seq=tiles_per_seq),
        grid=(ntok // tm,),
        in_specs=[tok(dm)[0], _resident((1, dm)), _layer_cols(w_in, layer, n_a, 0),
                  _layer_cols(w_in, layer, LANES, o_ab), _layer_cols(w_in, layer, n_b, o_b),
                  _layer_cols(w_in, layer, n_c, o_c), _layer_cols(w_in, layer, n_g // 2, o_g),
                  _layer_cols(w_in, layer, n_g // 2, o_g + n_g // 2),
                  _resident((1, half)), _resident((1, half)), _resident(ws.shape), _resident(bs.shape),
                  pl.BlockSpec((None, 8, c), seq_of), _resident(lw['conv_w'].shape)],
        out_specs=[o[0] for o in outs],
        out_shape=[o[1] for o in outs],
        scratch_shapes=[pltpu.VMEM((8, c), F32)],
        compiler_params=_cparams(("arbitrary",)),
        name="in_proj",
    )(h, norm_g, w_in, w_in, w_in, w_in, w_in, w_in, lw['ln_v_g'], lw['ln_v_b'], ws, bs,
      conv_buf8, lw['conv_w'])


def _conv_act(x, prev, cw, kind):
    n_tap = cw.shape[0]
    row8 = lax.broadcasted_iota(jnp.int32, (8, x.shape[1]), 0)
    y = x * cw[n_tap - 1:n_tap, :]
    for sh in range(1, n_tap):
        xs = pltpu.roll(x, sh, 0)
        head = jnp.where(row8 < sh, pltpu.roll(prev, sh, 0), xs[:8])
        xs = jnp.concatenate([head, xs[8:]], axis=0)
        y = y + xs * cw[n_tap - 1 - sh:n_tap - sh, :]
    y = _silu(y)
    if kind == 'v':
        return y
    parts = []
    for h in range(x.shape[1] // LANES):
        m = y[:, h * LANES:(h + 1) * LANES]
        m = m * lax.rsqrt(jnp.sum(m * m, axis=-1, keepdims=True) + EPS)
        parts.append(m * (LANES ** -0.5) if kind == 'q' else m)
    return jnp.concatenate(parts, axis=1)


def _qkv_groups(n_heads, c):
    w = n_heads * LANES
    return (('q', slice(0, w)), ('k', slice(w, 2 * w)), ('v', slice(2 * w, c)))


def _qkv_act(x, prev, cw, n_heads):
    return jnp.concatenate([_conv_act(x[:, sl], prev[:, sl], cw[:, sl], kind)
                            for kind, sl in _qkv_groups(n_heads, x.shape[1])], axis=1)


def _gdn_stream(x, prev, ab, z_all, s_pairs, s_next, cw, alog, dtb, gn, *, t_real, n_heads, n_double,
                pre_act):
    R = ROWS
    L2 = 2 * LANES
    qk_w = n_heads * LANES
    row = lax.broadcasted_iota(jnp.int32, (R, LANES), 0)
    valid = row < t_real
    row2 = lax.broadcasted_iota(jnp.int32, (R, L2), 0)
    col2 = lax.broadcasted_iota(jnp.int32, (R, L2), 1) & (LANES - 1)

    y = x if pre_act else _qkv_act(x, prev, cw, n_heads)

    g = -jnp.exp(alog) * _softplus(ab + dtb)
    beta = jax.nn.sigmoid(ab)
    if t_real != R:
        g = jnp.where(valid, g, 0.0)
        beta = jnp.where(valid, beta, 0.0)
    rm = row & (CHUNK - 1)
    gc = g
    sh = 1
    while sh < CHUNK:
        gc = gc + jnp.where(rm >= sh, pltpu.roll(gc, sh, 0), 0.0)
        sh *= 2
    gc_t = gc.T
    e_gc = jnp.exp(gc)
    g_last = jnp.where(row < CHUNK, gc[CHUNK - 1:CHUNK, :], gc[R - 1:R, :])
    e_rest = jnp.exp(g_last - gc)

    same_blk = (row2 & -CHUNK) == (col2 & -CHUNK)
    incl = same_blk & (row2 >= col2)
    strict = same_blk & (row2 > col2)
    eye = (row2 == col2).astype(F32)
    zeros_half = jnp.zeros((CHUNK, L2), F32)

    def cols(m, h0):
        return jnp.concatenate([jnp.broadcast_to(m[:, h0:h0 + 1], (R, LANES)),
                                jnp.broadcast_to(m[:, h0 + 1:h0 + 2], (R, LANES))], axis=1)

    def chain(p):
        h0 = 2 * p
        ps = slice(h0 * LANES, (h0 + 2) * LANES)
        qp = y[:, h0 * LANES:(h0 + 2) * LANES]
        kp = y[:, qk_w + h0 * LANES:qk_w + (h0 + 2) * LANES]
        vp = y[:, 2 * qk_w + h0 * LANES:2 * qk_w + (h0 + 2) * LANES]
        if t_real != R:
            ok = row2 < t_real
            qp, kp, vp = (jnp.where(ok, m, 0.0) for m in (qp, kp, vp))

        gcol = cols(gc, h0)
        grow = jnp.concatenate([gc_t[h0:h0 + 1, :], gc_t[h0 + 1:h0 + 2, :]], axis=1)
        bcol = cols(beta, n_heads + h0)
        ecol = cols(e_gc, h0)
        rcol = cols(e_rest, h0)

        decay = jnp.where(incl, jnp.exp(gcol - grow), 0.0)
        kq = _dot_nt(jnp.concatenate([kp, qp], axis=0), _blockdiag(kp))
        yield
        a_neg = jnp.where(strict, -(bcol * decay * kq[:R]), 0.0)
        p_mat = decay * kq[R:]

        t_inv = eye + a_neg
        pw = _pdot(a_neg, a_neg) if n_double else None
        yield
        for it in range(n_double):
            if it < n_double - 1:
                both = _pdot(jnp.concatenate([pw, t_inv], axis=0), pw)
                pw = both[:R]
                t_inv = t_inv + both[R:]
            else:
                t_inv = t_inv + _pdot(t_inv, pw)
            yield
        resid = (eye - t_inv) + _pdot3(a_neg, t_inv)
        yield
        t_inv = t_inv + _pdot(t_inv, resid)
        yield

        k_in = ecol * kp
        q_dec = ecol * qp
        k_dec = rcol * kp
        while s_pairs[p][0] is None:
            yield
        s_cur = s_pairs[p][0]
        outs = []
        n_live = -(-t_real // CHUNK)
        for c in range(n_live):
            rs = slice(c * CHUNK, (c + 1) * CHUNK)
            ks_qs = _pdot(jnp.concatenate([k_in[rs], q_dec[rs]], axis=0), s_cur)
            yield
            rhs = bcol[rs] * (vp[rs] - ks_qs[:CHUNK])
            parts = [zeros_half] * (R // CHUNK)
            parts[c] = rhs
            u = _pdot(t_inv[rs, :], jnp.concatenate(parts, axis=0))
            yield
            parts[c] = u
            u_full = jnp.concatenate(parts, axis=0)
            outs.append(ks_qs[CHUNK:] + _pdot(p_mat[rs, :], u_full))
            parts[c] = k_dec[rs]
            kd = jnp.concatenate(parts, axis=0)
            kd = jnp.concatenate([kd[:, :LANES], kd[:, LANES:]], axis=0)
            last = (c + 1) * CHUNK - 1
            bd = jnp.exp(jnp.concatenate([jnp.broadcast_to(gc[last:last + 1, h0:h0 + 1], (1, LANES)),
                                          jnp.broadcast_to(gc[last:last + 1, h0 + 1:h0 + 2], (1, LANES))],
                                         axis=1))
            s_cur = bd * s_cur + _dot_tn(kd, _blockdiag(u_full.astype(BF16)))
            yield
        o = jnp.concatenate(outs, axis=0)
        zg = _silu(z_all[:n_live * CHUNK, ps])
        s_next[p][0] = s_cur
        return jnp.concatenate([_rms(o[:, :LANES], gn), _rms(o[:, LANES:], gn)], axis=1) * zg

    return [chain(p) for p in range(n_heads // 2)]


def _lockstep(gens):
    done = [None] * len(gens)
    live = list(range(len(gens)))
    while live:
        for i in list(live):
            try:
                next(gens[i])
            except StopIteration as stop:
                done[i] = stop.value
                live.remove(i)
    return done


def _gdn_body(qkv_ref, z_ref, ab_ref, cbuf_ref, s0_ref, cw_ref, alog_ref, dtb_ref, gn_ref,
              y_ref, snew_ref, xprev_sc, s_sc, *, t_real, n_heads, n_double, pre_act):
    t = pl.program_id(1)
    nt = pl.num_programs(1)
    n_streams = qkv_ref.shape[0]
    n_pairs = n_heads // 2

    @pl.when(t == 0)
    def _():
        xprev_sc[...] = cbuf_ref[...]
        for s in range(n_streams):
            for p in range(n_pairs):
                s_sc[s, p] = jnp.concatenate([s0_ref[s, 2 * p], s0_ref[s, 2 * p + 1]], axis=1)

    def pad_rows(x):
        if t_real == ROWS:
            return x
        return jnp.concatenate([x, jnp.zeros((ROWS - t_real, x.shape[1]), x.dtype)], axis=0)

    n_tiles = qkv_ref.shape[1] // t_real
    chains, last = [], []
    for s in range(n_streams):
        state = [[s_sc[s, p]] for p in range(n_pairs)]
        for k in range(n_tiles):
            rs = slice(k * t_real, (k + 1) * t_real)
            x = pad_rows(qkv_ref[s, rs, :])
            nxt = [[None] for _ in range(n_pairs)]
            chains += _gdn_stream(x, xprev_sc[s], pad_rows(ab_ref[s, rs, :]), pad_rows(z_ref[s, rs, :]),
                                  state, nxt, cw_ref[...], alog_ref[...], dtb_ref[...], gn_ref[...],
                                  t_real=t_real, n_heads=n_heads, n_double=n_double, pre_act=pre_act)
            state = nxt
            if not pre_act:
                xprev_sc[s] = x[ROWS - 8:, :]
        last.append(state)
    done = iter(_lockstep(chains))
    for s in range(n_streams):
        for k in range(n_tiles):
            for p in range(n_pairs):
                y_ref[s, k * t_real:(k + 1) * t_real, 2 * p * LANES:2 * (p + 1) * LANES] = (
                    next(done)[:t_real].astype(y_ref.dtype))
        for p in range(n_pairs):
            s_sc[s, p] = last[s][p][0]

    @pl.when(t == nt - 1)
    def _():
        for s in range(n_streams):
            for p in range(n_pairs):
                sp = s_sc[s, p]
                snew_ref[s, 2 * p] = sp[:, :LANES]
                snew_ref[s, 2 * p + 1] = sp[:, LANES:]


def _gdn(aqkv, az, ab, conv_buf8, s0_arr, s0_index, lw, batch, seq, pre_act):
    n_heads = lw['n_heads']
    ns = GDN_STREAMS
    assert batch % ns == 0 and n_heads % 2 == 0
    t_real = min(ROWS, seq)
    step_rows = t_real * (GDN_TILES if seq % (ROWS * GDN_TILES) == 0 else 1)
    nt = seq // step_rows
    c = aqkv.shape[1]
    vw = az.shape[1]
    n_double = max(int(math.ceil(math.log2(min(CHUNK, seq)))) - 1, 0)
    body = functools.partial(_gdn_body, t_real=t_real, n_heads=n_heads, n_double=n_double, pre_act=pre_act)
    tok = lambda w: pl.BlockSpec((ns, step_rows, w), lambda b, t: (b, t, 0))
    s_shape = (ns, n_heads, LANES, LANES)
    return pl.pallas_call(
        body,
        grid=(batch // ns, nt),
        in_specs=[tok(c), tok(vw), tok(LANES),
                  pl.BlockSpec((ns, 8, c), lambda b, t: (b, 0, 0)),
                  pl.BlockSpec((None,) * (s0_arr.ndim - 4) + s_shape, s0_index),
                  _resident(lw['conv_w'].shape), _resident((1, LANES)), _resident((1, LANES)),
                  _resident((1, LANES))],
        out_specs=[tok(vw), pl.BlockSpec(s_shape, lambda b, t: (b, 0, 0, 0))],
        out_shape=[jax.ShapeDtypeStruct((batch, seq, vw), BF16),
                   jax.ShapeDtypeStruct((batch,) + s_shape[1:], F32)],
        scratch_shapes=[pltpu.VMEM((ns, 8, c), F32), pltpu.VMEM((ns, n_heads // 2, LANES, 2 * LANES), F32)],
        compiler_params=_cparams(("parallel", "arbitrary")),
        name="gdn",
    )(aqkv.reshape(batch, seq, c), az.reshape(batch, seq, vw), ab.reshape(batch, seq, LANES),
      conv_buf8, s0_arr, lw['conv_w'], lw['a_log'], lw['dt_bias'], lw['gdn_norm'])


def _sb_block_stages(q, kb, vb, carry, tri, mask):
    z = _dot_nt(q, kb) * (LANES ** -0.5)
    yield
    sp = _softplus(z)
    lf = -sp if mask is None else jnp.where(mask, -sp, 0.0)
    hi, lo = _split(lf)
    after = (jnp.dot(hi, tri, preferred_element_type=F32)
             + jnp.dot(lo, tri, preferred_element_type=F32))
    yield
    a = jnp.exp((z - sp) + after)
    if mask is not None:
        a = jnp.where(mask, a, 0.0)
    out = jnp.exp(carry) * _dot(a, vb)
    return out, jnp.sum(lf, axis=-1, keepdims=True)


def _sb_block(q, kb, vb, carry, tri, mask):
    return _lockstep([_sb_block_stages(q, kb, vb, carry, tri, mask)])[0]


def _sb_prompt_body(q_ref, k_ref, v_ref, o_ref, acc_sc, car_sc, *, n_heads):
    i = pl.program_id(1)
    n_seq, tq = q_ref.shape[:2]
    row = lax.broadcasted_iota(jnp.int32, (tq, tq), 0)
    col = lax.broadcasted_iota(jnp.int32, (tq, tq), 1)
    tri = (row > col).astype(BF16)
    chains = [(s, slice(h * LANES, (h + 1) * LANES)) for s in range(n_seq) for h in range(n_heads)]
    qs = [q_ref[s, :, sl] for s, sl in chains]

    def blocks(off, carries, mask):
        kb = [k_ref[s, pl.ds(off, tq), :] for s in range(n_seq)]
        vb = [v_ref[s, pl.ds(off, tq), :] for s in range(n_seq)]
        return _lockstep([_sb_block_stages(qs[c], kb[s][:, sl], vb[s][:, sl], carries[c], tri, mask)
                          for c, (s, sl) in enumerate(chains)])

    def fold(res, carries):
        live = None
        for c, (out, csum) in enumerate(res):
            if carries is None:
                acc_sc[c] = out
                carry = csum
            else:
                acc_sc[c] += out
                carry = carries[c] + csum
            car_sc[c] = carry
            m = jnp.max(carry)
            live = m if live is None else jnp.maximum(live, m)
        return live

    zero = jnp.zeros((tq, 1), F32)
    live = fold(blocks(pl.multiple_of(i * tq, tq), [zero] * len(chains), col < row), None)

    def cond(st):
        j, live = st
        return (j >= 0) & (live > SB_DEAD_LOG)

    def body(st):
        j, _ = st
        carries = [car_sc[c] for c in range(len(chains))]
        return j - 1, fold(blocks(pl.multiple_of(j * tq, tq), carries, None), carries)

    lax.while_loop(cond, body, (i - 1, live))
    for c, (s, sl) in enumerate(chains):
        o_ref[s, :, sl] = acc_sc[c].astype(o_ref.dtype)


def _sb_prompt(q, k, v, batch, seq, n_heads):
    tq = min(SB_PROMPT_BLOCK, seq)
    w = n_heads * LANES
    ns = SB_PROMPT_SEQS if batch % SB_PROMPT_SEQS == 0 else 1
    blk = pl.BlockSpec((ns, tq, w), lambda b, i: (b, i, 0))
    full = pl.BlockSpec((ns, seq, w), lambda b, i: (b, 0, 0), pipeline_mode=pl.Buffered(1))
    return pl.pallas_call(
        functools.partial(_sb_prompt_body, n_heads=n_heads),
        grid=(batch // ns, seq // tq),
        in_specs=[blk, full, full],
        out_specs=blk,
        out_shape=jax.ShapeDtypeStruct((batch, seq, w), BF16),
        scratch_shapes=[pltpu.VMEM((ns * n_heads, tq, LANES), F32), pltpu.VMEM((ns * n_heads, tq, 1), F32)],
        compiler_params=_cparams(("parallel", "arbitrary")),
        name="sb_prompt",
    )(q.reshape(batch, seq, w), k.reshape(batch, seq, w), v.reshape(batch, seq, w))


def _sb_sample_body(q_ref, k_ref, v_ref, kp_hbm, vp_hbm, o_ref, kbuf, vbuf, sem, acc_sc, car_sc,
                    *, n_heads, tk, layer):
    b = pl.program_id(0)
    n_streams = pl.num_programs(0)
    tq = q_ref.shape[0]
    blk_rows = tk * n_heads
    n_blk = kp_hbm.shape[2] // blk_rows
    newest = n_blk - 1

    def copies(stream, j, slot):
        src = pl.ds(j * blk_rows, blk_rows)
        return (pltpu.make_async_copy(kp_hbm.at[layer, stream, src, :], kbuf.at[slot], sem.at[0, slot]),
                pltpu.make_async_copy(vp_hbm.at[layer, stream, src, :], vbuf.at[slot], sem.at[1, slot]))

    def start(stream, j, slot):
        for c in copies(stream, j, slot):
            c.start()

    def wait(stream, j, slot):
        for c in copies(stream, j, slot):
            c.wait()

    def slot_of(j):
        return jnp.where(j == newest, 2 + lax.rem(b, 2), lax.rem(newest - 1 - j, 2))

    @pl.when(b == 0)
    def _():
        start(b, newest, 2)

    @pl.when(b + 1 < n_streams)
    def _():
        start(b + 1, newest, 2 + lax.rem(b + 1, 2))

    if n_blk > 1:
        start(b, newest - 1, 0)

    row = lax.broadcasted_iota(jnp.int32, (tq, LANES), 0)
    col = lax.broadcasted_iota(jnp.int32, (tq, LANES), 1)
    r2 = lax.broadcasted_iota(jnp.int32, (LANES, LANES), 0)
    c2 = lax.broadcasted_iota(jnp.int32, (LANES, LANES), 1)
    tri_new = (r2 > c2).astype(BF16)
    r3 = lax.broadcasted_iota(jnp.int32, (tk, tk), 0)
    c3 = lax.broadcasted_iota(jnp.int32, (tk, tk), 1)
    tri_past = (r3 > c3).astype(BF16)
    pad = jnp.zeros((LANES - tq, LANES), BF16)
    heads = [slice(h * LANES, (h + 1) * LANES) for h in range(n_heads)]
    qs = [q_ref[:, sl] for sl in heads]

    def fold(res, carries, first):
        live = None
        for h, (out, csum) in enumerate(res):
            if first:
                acc_sc[h] = out
                carry = csum
            else:
                acc_sc[h] += out
                carry = carries[h] + csum
            car_sc[h] = carry
            m = jnp.max(carry)
            live = m if live is None else jnp.maximum(live, m)
        return live

    zero = jnp.zeros((tq, 1), F32)
    live = fold(_lockstep([_sb_block_stages(qs[h], jnp.concatenate([k_ref[:, sl], pad], axis=0),
                                            jnp.concatenate([v_ref[:, sl], pad], axis=0),
                                            zero, tri_new, col < row)
                           for h, sl in enumerate(heads)]), None, True)

    def cond(st):
        j, live = st
        return (j >= 0) & (live > SB_DEAD_LOG)

    def body(st):
        j, _ = st
        slot = slot_of(j)
        wait(b, j, slot)

        @pl.when((j > 0) & (j < newest))
        def _():
            start(b, j - 1, slot_of(j - 1))

        carries = [car_sc[h] for h in range(n_heads)]
        res = _lockstep([_sb_block_stages(qs[h], kbuf[slot, pl.ds(h, tk, stride=n_heads), :],
                                          vbuf[slot, pl.ds(h, tk, stride=n_heads), :],
                                          carries[h], tri_past, None) for h in range(n_heads)])
        return j - 1, fold(res, carries, False)

    j_end, _ = lax.while_loop(cond, body, (newest, live))

    @pl.when(j_end == newest)
    def _():
        wait(b, newest, slot_of(newest))

    if n_blk > 1:
        @pl.when(j_end >= 0)
        def _():
            ahead = jnp.minimum(j_end, newest - 1)
            wait(b, ahead, slot_of(ahead))

    for h, sl in enumerate(heads):
        o_ref[:, sl] = acc_sc[h].astype(o_ref.dtype)


def _sb_sample(q, k, v, cache_k, cache_v, layer, batch, seq, n_heads):
    w = n_heads * LANES
    depth, _, past = cache_k.shape[:3]
    tk = min(SB_BLOCK, past)
    assert past % tk == 0
    new = pl.BlockSpec((None, seq, w), lambda b: (b, 0, 0))
    hbm = pl.BlockSpec(memory_space=pl.ANY)
    rows = lambda c: c.reshape(depth, batch, past * n_heads, LANES)
    return pl.pallas_call(
        functools.partial(_sb_sample_body, n_heads=n_heads, tk=tk, layer=layer),
        grid=(batch,),
        in_specs=[new, new, new, hbm, hbm],
        out_specs=new,
        out_shape=jax.ShapeDtypeStruct((batch, seq, w), BF16),
        scratch_shapes=[pltpu.VMEM((4, tk * n_heads, LANES), F32), pltpu.VMEM((4, tk * n_heads, LANES), F32),
                        pltpu.SemaphoreType.DMA((2, 4)),
                        pltpu.VMEM((n_heads, seq, LANES), F32), pltpu.VMEM((n_heads, seq, 1), F32)],
        compiler_params=_cparams(("arbitrary",)),
        name="sb_sample",
    )(q.reshape(batch, seq, w), k.reshape(batch, seq, w), v.reshape(batch, seq, w), rows(cache_k), rows(cache_v))


def _post_body(h_ref, gate_ref, ya_ref, yb_ref, yc_ref, p_ref,
               wpa_ref, wpb_ref, wpc_ref, wo_ref, nf_ref, wf1_ref, wf3_ref, wf2_ref,
               npl_ref, wpg_ref, wpp_ref, nfin_ref, o_ref, *, final, ffn_bounds):
    d = functools.partial(jnp.dot, preferred_element_type=F32)
    h = h_ref[...]
    dm = h.shape[1]
    gates = jax.nn.sigmoid(gate_ref[...])
    merged = (gates[:, :dm] * d(ya_ref[...], wpa_ref[...])
              + gates[:, dm:2 * dm] * d(yb_ref[...], wpb_ref[...])
              + gates[:, 2 * dm:] * d(yc_ref[...], wpc_ref[...]))
    h = h + d(merged.astype(BF16), wo_ref[...])
    xn = _rms(h, nf_ref[...]).astype(BF16)
    ff = None
    for lo, hi in zip(ffn_bounds[:-1], ffn_bounds[1:]):
        cs = slice(lo, hi)
        a1 = d(xn, wf1_ref[:, cs])
        a3 = d(xn, wf3_ref[:, cs])
        part = d((_silu(a1) * a3).astype(BF16), wf2_ref[cs, :])
        ff = part if ff is None else ff + part
    h = h + ff
    xn = _rms(h, npl_ref[...]).astype(BF16)
    h = h + jax.nn.sigmoid(d(xn, wpg_ref[...])) * d(p_ref[...].astype(BF16), wpp_ref[...])
    if final:
        h = _rms(h, nfin_ref[...])
    o_ref[...] = h


def _post(h, gate, ya, yb, yc, p_arr, layer, lw, wts, norm_final, final, tm):
    ntok, dm = h.shape
    tm = min(tm, ntok)
    bw = ya.shape[1]
    pd = p_arr.shape[-1]
    p3 = p_arr.reshape(p_arr.shape[0], ntok, pd)
    tok = lambda w: pl.BlockSpec((tm, w), lambda i: (i, 0))
    hid = wts['w_ffn_out'].shape[1]
    whole = lambda name: (_layer_cols(wts[name], layer), wts[name])
    vec = lambda x: (_resident(x.shape), x)
    weights = [whole('w_pa'), whole('w_pb'), whole('w_pc'), whole('w_o'), vec(lw['norm_ffn']),
               (_layer_cols(wts['w_ffn_in'], layer, hid, 0), wts['w_ffn_in']),
               (_layer_cols(wts['w_ffn_in'], layer, hid, hid), wts['w_ffn_in']),
               whole('w_ffn_out'), vec(lw['norm_ple']), whole('w_ple_gate'), whole('w_ple_proj'),
               vec(norm_final)]
    mid = -(-hid // (2 * MXU_WIDTH)) * MXU_WIDTH if hid % MXU_WIDTH == 0 else hid // 2
    return pl.pallas_call(
        functools.partial(_post_body, final=final, ffn_bounds=(0, mid, hid)),
        grid=(ntok // tm,),
        in_specs=[tok(dm), tok(gate.shape[1]), tok(bw), tok(bw), tok(bw),
                  pl.BlockSpec((None, tm, pd), lambda i: (layer, i, 0))]
                 + [w[0] for w in weights],
        out_specs=tok(dm),
        out_shape=jax.ShapeDtypeStruct((ntok, dm), F32),
        compiler_params=_cparams(("parallel",)),
        name="post",
    )(h, gate, ya, yb, yc, p3, *[w[1] for w in weights])


def _pack_weights(n_heads, n_a, w_in, w_pa, w_pb, w_pc, w_o, w_ffn_in, w_ffn_out, w_ple_gate, w_ple_proj):
    n_ab = 2 * n_heads
    cast = lambda w: w.astype(BF16)
    w16 = cast(w_in)
    w_in_p = jnp.concatenate([w16[:, :, :n_a], w16[:, :, n_a + n_ab:],
                              jnp.pad(w16[:, :, n_a:n_a + n_ab], ((0, 0), (0, 0), (0, LANES - n_ab)))], axis=2)
    return dict(w_in=w_in_p, w_pa=cast(w_pa), w_pb=cast(w_pb), w_pc=cast(w_pc), w_o=cast(w_o),
                w_ffn_in=cast(w_ffn_in), w_ffn_out=cast(w_ffn_out), w_ple_gate=cast(w_ple_gate),
                w_ple_proj=cast(w_ple_proj))


def _layer_weights(i, a_log, dt_bias, w_in, conv_w, norm_mix, gdn_norm, ln_v_g, ln_v_b, w_s, b_s, w_pc,
                   norm_ffn, norm_ple, dec_seq):
    n_heads = a_log.shape[1]
    c = conv_w.shape[2]
    vw = (c - 2 * n_heads * LANES)
    n_uv = 2 * w_s.shape[1] * LANES
    n_sb = w_pc.shape[1]
    n_gate = w_in.shape[2] - (c + vw) - 2 * n_heads - n_uv - 3 * n_sb
    row = lambda x: x.reshape(1, -1).astype(F32)
    lane_pad = lambda x: jnp.pad(x.reshape(1, -1), ((0, 0), (0, LANES - x.size))).astype(F32)
    span = w_s.shape[2]
    rep = ROWS // dec_seq
    lw = dict(
        n_heads=n_heads, n_qkv=c, n_z=vw, n_uv=n_uv, n_sb=n_sb, n_gate=n_gate,
        norm_mix=row(norm_mix[i]),
        conv_w=conv_w[i].astype(F32), a_log=lane_pad(a_log[i]), dt_bias=lane_pad(dt_bias[i]),
        gdn_norm=row(gdn_norm[i]),
        ln_v_g=row(ln_v_g[i]), ln_v_b=row(ln_v_b[i]),
        ws_p=w_s[i].astype(F32),
        bs_p=jnp.repeat(b_s[i].T, LANES, axis=1).astype(F32),
        ws_s=jnp.tile(w_s[i][:, :dec_seq, :dec_seq], (1, rep, rep)).astype(F32),
        bs_s=jnp.tile(jnp.repeat(b_s[i][:, :dec_seq].T, LANES, axis=1), (rep, 1)).astype(F32),
        norm_ffn=row(norm_ffn[i]), norm_ple=row(norm_ple[i]),
    )
    assert span == ROWS
    return lw


def _layer(h, p_arr, layer, conv_buf8, s0_arr, s0_index, caches, lw, wts, norm_final, final, batch, seq, tm):
    n_heads = lw['n_heads']
    sample = caches is not None
    aqkv, az, ab, yb, qb, kb, vb, k4, v4, gate, extra = _in_proj(
        h, lw['norm_mix'], lw, wts, layer, tm, batch, seq, conv_buf8)
    pre_act = extra.ndim == 3
    v_rows = None if pre_act else extra
    ya, s_new = _gdn(aqkv, az, ab, conv_buf8, s0_arr, s0_index, lw, batch, seq, pre_act)
    conv_new = extra[:, 5:, :] if pre_act else aqkv.reshape(batch, seq, -1)[:, seq - 3:, :]
    if sample:
        yc = _sb_sample(qb, kb, vb, caches[0], caches[1], layer, batch, seq, n_heads)
    else:
        yc = _sb_prompt(qb, kb, vb, batch, seq, n_heads)
    bw = ya.shape[-1]
    h = _post(h, gate, ya.reshape(-1, bw), yb, yc.reshape(-1, bw), p_arr, layer, lw, wts, norm_final, final,
              tm)
    kv_shape = (batch, seq, n_heads, LANES)
    return h, conv_new, s_new, k4.reshape(kv_shape), v4.reshape(kv_shape), v_rows


def kernel(x_prompt, x_sample, state_gdn_conv, state_gdn_s, cache_sb_k, cache_sb_v, p_prompt, p_sample, norm_mix, w_in, conv_w, a_log, dt_bias, gdn_norm, w_pa, ln_v_g, ln_v_b, w_s, b_s, w_pb, w_pc, w_o, norm_ffn, w_ffn_in, w_ffn_out, norm_ple, w_ple_gate, w_ple_proj, norm_final):
    depth = w_in.shape[0]
    bp, sp, dm = x_prompt.shape
    bs, ss, _ = x_sample.shape
    n_heads = a_log.shape[1]
    hp = x_prompt.reshape(bp * sp, dm)
    hs = x_sample.reshape(bs * ss, dm)
    nfin = norm_final.reshape(1, dm).astype(F32)
    c = conv_w.shape[2]
    zero_buf = jnp.zeros((bp, 8, c), F32)
    zero_s = jnp.zeros((bp, n_heads, LANES, LANES), F32)
    outs = [[] for _ in range(9)]
    wts = _pack_weights(n_heads, c + (c - 2 * n_heads * LANES), w_in, w_pa, w_pb, w_pc, w_o, w_ffn_in,
                        w_ffn_out, w_ple_gate, w_ple_proj)
    for i in range(depth):
        lw = _layer_weights(i, a_log, dt_bias, w_in, conv_w, norm_mix, gdn_norm, ln_v_g, ln_v_b, w_s, b_s, w_pc,
                            norm_ffn, norm_ple, ss)
        final = i == depth - 1
        hp, pc, ps, pk, pv, _ = _layer(hp, p_prompt, i, zero_buf, zero_s, lambda b, t: (b, 0, 0, 0), None,
                                       lw, wts, nfin, final, bp, sp, 256)
        buf8 = jnp.pad(state_gdn_conv[i], ((0, 0), (8 - state_gdn_conv.shape[2], 0), (0, 0)))
        hs, sc, sn, sk, sv, sm = _layer(hs, p_sample, i, buf8, state_gdn_s,
                                        lambda b, t, i=i: (i, b, 0, 0, 0), (cache_sb_k, cache_sb_v),
                                        lw, wts, nfin, final, bs, ss, 256)
        for lst, val in zip(outs, (pc, ps, pk, pv, sc, sn, sk, sv, sm.reshape(bs, ss, -1))):
            lst.append(val)
    stacked = [jnp.stack(l) for l in outs]
    return (hp.reshape(bp, sp, dm), hs.reshape(bs, ss, dm), *stacked)
```
